```python
import math
import jax
import jax.numpy as jnp
from jax import lax
import numpy as np

D_MODEL = 1024
BATCH = 32
SEQ = 2048
DEPTH = 4
DEC_BATCH = 2
DEC_SEQ = 16384
PAST_LEN = 128

N_MIXERS = 2
N_ATT_LAYERS = (DEPTH + 1) // 2
N_SSM_LAYERS = DEPTH // 2
N_HEADS = 16
QK_NOPE_DIM = 64
QK_ROPE_DIM = 32
V_HEAD_DIM = 64
Q_LORA_RANK = 256
KV_LORA_RANK = 128
ROPE_THETA = 10000.0
Q_BLOCK = 128
ATTN_SCALE = (QK_NOPE_DIM + QK_ROPE_DIM) ** -0.5
GROUP_CH = 16
N_GROUPS = D_MODEL // GROUP_CH
STATE_DIM = 64
D_FF = 2816
N_EXPERTS = 8
TOP_K = 2
D_FF_EXPERT = 3584
NORM_EPS = 1e-6

kernel_name = 'hybrid_mla_s5_moe_encoder'


def _rmsnorm(x, g):
    xf = x.astype(jnp.float32)
    y = xf * lax.rsqrt(jnp.mean(xf * xf, axis=-1, keepdims=True) + NORM_EPS)
    return (y * g.astype(jnp.float32)).astype(x.dtype)


def _rope_tables(seq_len):
    pos = jnp.arange(seq_len, dtype=jnp.float32)
    inv = ROPE_THETA ** (-jnp.arange(0, QK_ROPE_DIM, 2, dtype=jnp.float32) / QK_ROPE_DIM)
    ang = pos[:, None] * inv[None, :]
    ang = jnp.concatenate([ang, ang], axis=-1)
    return jnp.cos(ang), jnp.sin(ang)


def _apply_rope(t, cos, sin):
    tf = t.astype(jnp.float32)
    half = QK_ROPE_DIM // 2
    rot = jnp.concatenate([-tf[..., half:], tf[..., :half]], axis=-1)
    return (tf * cos + rot * sin).astype(t.dtype)


def _mla(h, w_dq, q_norm, w_uq, w_dkv, kv_norm, w_ukv, w_o):
    B, S, _ = h.shape
    cq = _rmsnorm(h @ w_dq, q_norm)
    q = (cq @ w_uq).reshape(B, S, N_HEADS, QK_NOPE_DIM + QK_ROPE_DIM)
    q_nope, q_rope = q[..., :QK_NOPE_DIM], q[..., QK_NOPE_DIM:]
    ckv_kr = h @ w_dkv
    c_kv = _rmsnorm(ckv_kr[..., :KV_LORA_RANK], kv_norm)
    k_rope = ckv_kr[..., KV_LORA_RANK:]
    kv = (c_kv @ w_ukv).reshape(B, S, N_HEADS, QK_NOPE_DIM + V_HEAD_DIM)
    k_nope, v = kv[..., :QK_NOPE_DIM], kv[..., QK_NOPE_DIM:]
    cos, sin = _rope_tables(S)
    q_rope = _apply_rope(q_rope, cos[:, None, :], sin[:, None, :])
    k_rope = _apply_rope(k_rope, cos, sin)
    qb = Q_BLOCK if S % Q_BLOCK == 0 else S
    nb = S // qb

    def to_blocks(t):
        return jnp.moveaxis(t.reshape((B, nb, qb) + t.shape[2:]), 1, 0)

    def attend(blk):
        qn, qr = blk
        s = (jnp.einsum('bqhd,bkhd->bhqk', qn, k_nope)
             + jnp.einsum('bqhr,bkr->bhqk', qr, k_rope))
        p = jax.nn.softmax(s.astype(jnp.float32) * ATTN_SCALE, axis=-1).astype(v.dtype)
        return jnp.einsum('bhqk,bkhd->bqhd', p, v)

    o = lax.map(attend, (to_blocks(q_nope), to_blocks(q_rope)))
    o = jnp.moveaxis(o, 0, 1).reshape(B, S, N_HEADS * V_HEAD_DIM)
    return o @ w_o


def _ssm_combine(left, right):
    a_l, b_l = left
    a_r, b_r = right
    return a_r * a_l, a_r * b_l + b_r


def _s5(h, a_re, a_im, log_step, b_re, b_im, c_re, c_im, d, w_glu):
    B, S, _ = h.shape
    hf = h.astype(jnp.float32)
    u = hf.reshape(B, S, N_GROUPS, GROUP_CH).astype(jnp.complex64)

    def direction(k, reverse):
        lam = lax.complex(a_re[k].astype(jnp.float32), a_im[k].astype(jnp.float32))
        delta = jnp.exp(log_step[k].astype(jnp.float32))[:, None]
        lam_bar = jnp.exp(lam * delta)
        b = lax.complex(b_re[k].astype(jnp.float32), b_im[k].astype(jnp.float32))
        b_bar = ((lam_bar - 1.0) / lam)[..., None] * b
        c = lax.complex(c_re[k].astype(jnp.float32), c_im[k].astype(jnp.float32))
        bu = jnp.einsum('bsgc,gpc->bsgp', u, b_bar)
        a = jnp.broadcast_to(lam_bar, (1, S) + lam_bar.shape)
        _, state = lax.associative_scan(_ssm_combine, (a, bu), reverse=reverse, axis=1)
        return jnp.einsum('bsgp,gcp->bsgc', state, c).real

    y = direction(0, False) + direction(1, True)
    y = y.reshape(B, S, D_MODEL) + d.astype(jnp.float32) * hf
    g = jax.nn.gelu(y).astype(h.dtype)
    z = g @ w_glu
    return z[..., :D_MODEL] * jax.nn.sigmoid(z[..., D_MODEL:])


def _swiglu(h, w_gate, w_up, w_down):
    return (jax.nn.silu(h @ w_gate) * (h @ w_up)) @ w_down


def _moe(h, w_router, w_gate, w_up, w_down):
    logits = jnp.einsum('bsd,de->bse', h, w_router).astype(jnp.float32)
    top_vals, top_idx = lax.top_k(logits, TOP_K)
    gates = jax.nn.softmax(top_vals, axis=-1)
    combine = jnp.sum(gates[..., None] * jax.nn.one_hot(top_idx, N_EXPERTS, dtype=jnp.float32), axis=-2)
    combine = combine.astype(h.dtype)
    out = jnp.zeros_like(h)
    for e in range(N_EXPERTS):
        out = out + combine[..., e:e + 1] * _swiglu(h, w_gate[e], w_up[e], w_down[e])
    return out


def _trunk(x, norm_mix, norm_ffn, norm_final,
           mla_w_dq, mla_q_norm, mla_w_uq, mla_w_dkv, mla_kv_norm, mla_w_ukv, mla_w_o,
           ssm_a_re, ssm_a_im, ssm_log_step, ssm_b_re, ssm_b_im, ssm_c_re, ssm_c_im, ssm_d, ssm_w_glu,
           ffn_w_gate, ffn_w_up, ffn_w_down,
           moe_w_router, moe_w_gate, moe_w_up, moe_w_down):
    for i in range(DEPTH):
        j = i // N_MIXERS
        h = _rmsnorm(x, norm_mix[i])
        if i % N_MIXERS == 0:
            x = x + _mla(h, mla_w_dq[j], mla_q_norm[j], mla_w_uq[j], mla_w_dkv[j],
                         mla_kv_norm[j], mla_w_ukv[j], mla_w_o[j])
        else:
            x = x + _s5(h, ssm_a_re[j], ssm_a_im[j], ssm_log_step[j], ssm_b_re[j], ssm_b_im[j],
                        ssm_c_re[j], ssm_c_im[j], ssm_d[j], ssm_w_glu[j])
        h = _rmsnorm(x, norm_ffn[i])
        if i % 2 == 0:
            x = x + _swiglu(h, ffn_w_gate[j], ffn_w_up[j], ffn_w_down[j])
        else:
            x = x + _moe(h, moe_w_router[j], moe_w_gate[j], moe_w_up[j], moe_w_down[j])
    return _rmsnorm(x, norm_final)


def setup_inputs(seed: int = 0) -> dict:
    key = jax.random.key(seed)
    ks = iter(jax.random.split(key, 32))
    f32 = jnp.float32

    def nrm(shape, scale):
        return jax.random.normal(next(ks), shape, f32) * scale

    def gain(shape):
        return 1.0 + nrm(shape, 0.02)

    NA, NS, G, P, GC = N_ATT_LAYERS, N_SSM_LAYERS, N_GROUPS, STATE_DIM, GROUP_CH
    D = D_MODEL
    x_prompt = nrm((BATCH, SEQ, D), 1.0)
    x_sample = nrm((DEC_BATCH, DEC_SEQ, D), 1.0)
    norm_mix = gain((DEPTH, D))
    norm_ffn = gain((DEPTH, D))
    norm_final = gain((D,))
    mla_w_dq = nrm((NA, D, Q_LORA_RANK), D ** -0.5)
    mla_q_norm = gain((NA, Q_LORA_RANK))
    mla_w_uq = nrm((NA, Q_LORA_RANK, N_HEADS * (QK_NOPE_DIM + QK_ROPE_DIM)), Q_LORA_RANK ** -0.5)
    mla_w_dkv = nrm((NA, D, KV_LORA_RANK + QK_ROPE_DIM), D ** -0.5)
    mla_kv_norm = gain((NA, KV_LORA_RANK))
    mla_w_ukv = nrm((NA, KV_LORA_RANK, N_HEADS * (QK_NOPE_DIM + V_HEAD_DIM)), KV_LORA_RANK ** -0.5)
    mla_w_o = nrm((NA, N_HEADS * V_HEAD_DIM, D), (N_HEADS * V_HEAD_DIM) ** -0.5)
    ssm_a_re = -0.5 * jnp.exp(nrm((NS, 2, G, P), 0.05))
    ssm_a_im = math.pi * jnp.arange(P, dtype=f32) + nrm((NS, 2, G, P), 0.02)
    ssm_log_step = jax.random.uniform(next(ks), (NS, 2, G), f32, math.log(1e-3), math.log(1e-1))
    ssm_b_re = nrm((NS, 2, G, P, GC), (2 * GC) ** -0.5)
    ssm_b_im = nrm((NS, 2, G, P, GC), (2 * GC) ** -0.5)
    ssm_c_re = nrm((NS, 2, G, GC, P), (2 * P) ** -0.5)
    ssm_c_im = nrm((NS, 2, G, GC, P), (2 * P) ** -0.5)
    ssm_d = nrm((NS, D), 1.0)
    ssm_w_glu = nrm((NS, D, 2 * D), D ** -0.5)
    ffn_w_gate = nrm((NA, D, D_FF), D ** -0.5)
    ffn_w_up = nrm((NA, D, D_FF), D ** -0.5)
    ffn_w_down = nrm((NA, D_FF, D), D_FF ** -0.5)
    moe_w_router = nrm((NS, D, N_EXPERTS), D ** -0.5)
    moe_w_gate = nrm((NS, N_EXPERTS, D, D_FF_EXPERT), D ** -0.5)
    moe_w_up = nrm((NS, N_EXPERTS, D, D_FF_EXPERT), D ** -0.5)
    moe_w_down = nrm((NS, N_EXPERTS, D_FF_EXPERT, D), D_FF_EXPERT ** -0.5)
    return {'x_prompt': x_prompt, 'x_sample': x_sample,
            'norm_mix': norm_mix, 'norm_ffn': norm_ffn, 'norm_final': norm_final,
            'mla_w_dq': mla_w_dq, 'mla_q_norm': mla_q_norm, 'mla_w_uq': mla_w_uq,
            'mla_w_dkv': mla_w_dkv, 'mla_kv_norm': mla_kv_norm, 'mla_w_ukv': mla_w_ukv,
            'mla_w_o': mla_w_o,
            'ssm_a_re': ssm_a_re, 'ssm_a_im': ssm_a_im, 'ssm_log_step': ssm_log_step,
            'ssm_b_re': ssm_b_re, 'ssm_b_im': ssm_b_im, 'ssm_c_re': ssm_c_re, 'ssm_c_im': ssm_c_im,
            'ssm_d': ssm_d, 'ssm_w_glu': ssm_w_glu,
            'ffn_w_gate': ffn_w_gate, 'ffn_w_up': ffn_w_up, 'ffn_w_down': ffn_w_down,
            'moe_w_router': moe_w_router, 'moe_w_gate': moe_w_gate, 'moe_w_up': moe_w_up,
            'moe_w_down': moe_w_down}


def reference(x_prompt, x_sample, norm_mix, norm_ffn, norm_final,
              mla_w_dq, mla_q_norm, mla_w_uq, mla_w_dkv, mla_kv_norm, mla_w_ukv, mla_w_o,
              ssm_a_re, ssm_a_im, ssm_log_step, ssm_b_re, ssm_b_im, ssm_c_re, ssm_c_im, ssm_d, ssm_w_glu,
              ffn_w_gate, ffn_w_up, ffn_w_down,
              moe_w_router, moe_w_gate, moe_w_up, moe_w_down):
    params = (norm_mix, norm_ffn, norm_final,
              mla_w_dq, mla_q_norm, mla_w_uq, mla_w_dkv, mla_kv_norm, mla_w_ukv, mla_w_o,
              ssm_a_re, ssm_a_im, ssm_log_step, ssm_b_re, ssm_b_im, ssm_c_re, ssm_c_im, ssm_d, ssm_w_glu,
              ffn_w_gate, ffn_w_up, ffn_w_down,
              moe_w_router, moe_w_gate, moe_w_up, moe_w_down)
    y_prompt = _trunk(x_prompt, *params)
    y_sample = _trunk(x_sample, *params)
    return (y_prompt, y_sample)
```

```python
import functools
import math

import jax
import jax.numpy as jnp
from jax import lax
from jax.experimental import pallas as pl
from jax.experimental.pallas import tpu as pltpu

BF16 = jnp.bfloat16
F32 = jnp.float32

N_HEADS = 16
QK_NOPE = 64
QK_ROPE = 32
V_DIM = 64
HEAD_PAD = 128
ROPE_THETA = 10000.0
ATTN_SCALE = (QK_NOPE + QK_ROPE) ** -0.5
LOG2E = 1.4426950408889634
GROUP_CH = 16
STATE_DIM = 64
N_EXPERTS = 8
NORM_EPS = 1e-6
LANES = 128
SUBLANES = 8
MXU_DIM = 256
VMEM_LIMIT = 56 * 1024 * 1024
NEG_BIG = -1e30


def _cparams(*sem):
    return pltpu.CompilerParams(dimension_semantics=sem, vmem_limit_bytes=VMEM_LIMIT)


def _resident(shape):
    zeros = (0,) * len(shape)
    return pl.BlockSpec(shape, lambda *_: zeros, pipeline_mode=pl.Buffered(1))


def _tile(n, cap, mult=SUBLANES):
    if n <= cap:
        return n
    t = (cap // mult) * mult
    while t > mult and n % t:
        t -= mult
    assert n % t == 0, (n, cap, mult)
    return t


def _rms(x, g):
    ms = jnp.mean(x * x, axis=-1, keepdims=True)
    return x * lax.rsqrt(ms + NORM_EPS) * g


def _dot(a, b):
    return jnp.dot(a, b, preferred_element_type=F32)


def _dot_nt(a, b):
    return lax.dot_general(a, b, (((1,), (1,)), ((), ())), preferred_element_type=F32)


def _dot_tn(a, b):
    return lax.dot_general(a, b, (((0,), (0,)), ((), ())), preferred_element_type=F32)


def _swiglu(h, wg_ref, wu_ref, wd_ref, act_ref, fc):
    f = act_ref.shape[1]
    for c0 in range(0, f, fc):
        g = _dot(h, wg_ref[:, c0:c0 + fc])
        u = _dot(h, wu_ref[:, c0:c0 + fc])
        act_ref[:, c0:c0 + fc] = (g * jax.nn.sigmoid(g) * u).astype(BF16)
    return _dot(act_ref[...], wd_ref[...])


def _mla_pre_kernel(x_ref, g_ref, wdq_ref, qn_ref, w1_ref, w2_ref, wdkv_ref, kvn_ref, wk_ref, wv_ref,
                    ctq_ref, stq_ref, ctk_ref, stk_ref, qt_ref, k_ref, vt_ref):
    h = _rms(x_ref[0], g_ref[...]).astype(BF16)
    cq = _rms(_dot(h, wdq_ref[...]), qn_ref[...]).astype(BF16)
    a = _dot_nt(w1_ref[...], cq)
    b = _dot_nt(w2_ref[...], cq)
    ct = ctq_ref[...]
    st = stq_ref[...]
    for hh in range(N_HEADS):
        sl = slice(hh * HEAD_PAD, (hh + 1) * HEAD_PAD)
        qt_ref[0, sl, :] = (a[sl] * ct + b[sl] * st).astype(BF16)
    ck = _dot(h, wdkv_ref[...])
    kvl = kvn_ref.shape[1]
    ckv = _rms(ck[:, :kvl], kvn_ref[...]).astype(BF16)
    kr = (ck[:, kvl:kvl + LANES] * ctk_ref[...] + ck[:, kvl + LANES:] * stk_ref[...]).astype(BF16)
    kaug = jnp.concatenate([ckv, kr], axis=1)
    k_ref[0] = _dot(kaug, wk_ref[...]).astype(BF16)
    vt_ref[0] = _dot_nt(wv_ref[...], ckv).astype(BF16)


def _mla_pre(x, g, w, tabs, tm):
    b, s, d = x.shape
    hq = N_HEADS * HEAD_PAD
    hv = N_HEADS * V_DIM
    ctq, stq, ctk, stk = tabs
    grid = (b, s // tm)
    in_specs = [
        pl.BlockSpec((1, tm, d), lambda bi, i: (bi, i, 0)),
        _resident(g.shape), _resident(w['wdq'].shape), _resident(w['qn'].shape),
        _resident(w['w1t'].shape), _resident(w['w2t'].shape), _resident(w['wdkv'].shape),
        _resident(w['kvn'].shape), _resident(w['wk'].shape), _resident(w['wvt'].shape),
        pl.BlockSpec((HEAD_PAD, tm), lambda bi, i: (0, i)),
        pl.BlockSpec((HEAD_PAD, tm), lambda bi, i: (0, i)),
        pl.BlockSpec((tm, LANES), lambda bi, i: (i, 0)),
        pl.BlockSpec((tm, LANES), lambda bi, i: (i, 0)),
    ]
    out_specs = [
        pl.BlockSpec((1, hq, tm), lambda bi, i: (bi, 0, i)),
        pl.BlockSpec((1, tm, hq), lambda bi, i: (bi, i, 0)),
        pl.BlockSpec((1, hv, tm), lambda bi, i: (bi, 0, i)),
    ]
    out_shape = [jax.ShapeDtypeStruct((b, hq, s), BF16), jax.ShapeDtypeStruct((b, s, hq), BF16),
                 jax.ShapeDtypeStruct((b, hv, s), BF16)]
    return pl.pallas_call(
        _mla_pre_kernel, grid=grid, in_specs=in_specs, out_specs=out_specs, out_shape=out_shape,
        compiler_params=_cparams("parallel", "parallel"), name="mla_pre",
    )(x, g, w['wdq'], w['qn'], w['w1t'], w['w2t'], w['wdkv'], w['kvn'], w['wk'], w['wvt'], ctq, stq, ctk, stk)


def _attn_kernel(qt_ref, k_ref, vt_ref, o_ref, *, tk):
    qt = qt_ref[0]
    tq = qt.shape[1]
    nk = k_ref.shape[1] // tk

    def body(j, carry):
        m, l, acc = carry
        k0 = pl.multiple_of(j * tk, tk)
        s = _dot(k_ref[0, pl.ds(k0, tk), :], qt)
        m_new = jnp.maximum(m, jnp.max(s, axis=0, keepdims=True))
        alpha = jnp.exp2(m - m_new)
        p = jnp.exp2(s - m_new)
        l = alpha * l + jnp.sum(p, axis=0, keepdims=True)
        acc = alpha * acc + _dot(vt_ref[0, :, pl.ds(k0, tk)], p.astype(BF16))
        return m_new, l, acc

    init = (jnp.full((1, tq), NEG_BIG, F32), jnp.zeros((1, tq), F32), jnp.zeros((V_DIM, tq), F32))
    _, l, acc = lax.fori_loop(0, nk, body, init)
    o_ref[0] = (acc / l).astype(BF16)


def _attention(qt, k, vt, tq, tk):
    b, hq, s = qt.shape
    grid = (b, N_HEADS, s // tq)
    return pl.pallas_call(
        functools.partial(_attn_kernel, tk=tk), grid=grid,
        in_specs=[pl.BlockSpec((1, HEAD_PAD, tq), lambda bi, h, i: (bi, h, i)),
                  pl.BlockSpec((1, s, HEAD_PAD), lambda bi, h, i: (bi, 0, h)),
                  pl.BlockSpec((1, V_DIM, s), lambda bi, h, i: (bi, h, 0))],
        out_specs=pl.BlockSpec((1, V_DIM, tq), lambda bi, h, i: (bi, h, i)),
        out_shape=jax.ShapeDtypeStruct((b, N_HEADS * V_DIM, s), BF16),
        compiler_params=_cparams("parallel", "parallel", "parallel"), name="mla_attention",
    )(qt, k, vt)


def _attn_out_ffn_kernel(x_ref, ot_ref, wo_ref, g_ref, wg_ref, wu_ref, wd_ref, y_ref, act_ref, *, fc):
    x1 = x_ref[0] + _dot_tn(ot_ref[0], wo_ref[...])
    h = _rms(x1, g_ref[...]).astype(BF16)
    y_ref[0] = x1 + _swiglu(h, wg_ref, wu_ref, wd_ref, act_ref, fc)


def _attn_out_ffn(x, ot, wo, g, wg, wu, wd, tm):
    b, s, d = x.shape
    f = wg.shape[1]
    fc = _tile(f, 1536, LANES)
    hv = ot.shape[1]
    return pl.pallas_call(
        functools.partial(_attn_out_ffn_kernel, fc=fc), grid=(b, s // tm),
        in_specs=[pl.BlockSpec((1, tm, d), lambda bi, i: (bi, i, 0)),
                  pl.BlockSpec((1, hv, tm), lambda bi, i: (bi, 0, i)),
                  _resident(wo.shape), _resident(g.shape), _resident(wg.shape), _resident(wu.shape),
                  _resident(wd.shape)],
        out_specs=pl.BlockSpec((1, tm, d), lambda bi, i: (bi, i, 0)),
        out_shape=jax.ShapeDtypeStruct((b, s, d), F32),
        scratch_shapes=[pltpu.VMEM((tm, f), BF16)],
        compiler_params=_cparams("parallel", "parallel"), name="attn_out_ffn",
    )(x, ot, wo, g, wg, wu, wd)


def _s5_param_kernel(are_ref, aim_ref, ls_ref, bre_ref, bim_ref, lr_ref, li_ref, bbr_ref, bbi_ref):
    ar = are_ref[...]
    ai = aim_ref[...]
    delta = jnp.exp(ls_ref[...])
    mag = jnp.exp(ar * delta)
    lr = mag * jnp.cos(ai * delta)
    li = mag * jnp.sin(ai * delta)
    nr = lr - 1.0
    den = ar * ar + ai * ai
    cr = (nr * ar + li * ai) / den
    ci = (li * ar - nr * ai) / den
    br = bre_ref[...]
    bi = bim_ref[...]
    lr_ref[...] = lr
    li_ref[...] = li
    bbr_ref[...] = cr * br - ci * bi
    bbi_ref[...] = cr * bi + ci * br


def _s5_scan_kernel(x_ref, g_ref, bt_ref, ct_ref, lr_ref, li_ref, init_ref, *rest, tt, reverse, emit_y):
    if emit_y:
        y_ref, fin_ref, bu_ref, st_ref = rest
    else:
        fin_ref, bu_ref, st_ref = rest
    c = pl.program_id(1)
    ntile = bt_ref.shape[0]
    half = bt_ref.shape[2] // 2

    @pl.when(c == 0)
    def _():
        st_ref[...] = init_ref[0]

    hb = _rms(x_ref[0], g_ref[...]).astype(BF16)
    for j in range(ntile):
        bu_ref[...] = _dot(hb[:, j * MXU_DIM:(j + 1) * MXU_DIM], bt_ref[j])
        lr = jnp.broadcast_to(lr_ref[j], (SUBLANES, half))
        li = jnp.broadcast_to(li_ref[j], (SUBLANES, half))

        def step(t, carry, lr=lr, li=li):
            xr, xi = carry
            tau = (tt - 1 - t) if reverse else t
            r0 = pl.multiple_of(tau * SUBLANES, SUBLANES)
            nr = lr * xr - li * xi + bu_ref[pl.ds(r0, SUBLANES), 0:half]
            ni = lr * xi + li * xr + bu_ref[pl.ds(r0, SUBLANES), half:2 * half]
            bu_ref[pl.ds(r0, SUBLANES), 0:half] = nr
            bu_ref[pl.ds(r0, SUBLANES), half:2 * half] = ni
            return nr, ni

        xr, xi = lax.fori_loop(0, tt, step, (st_ref[j, 0], st_ref[j, 1]))
        st_ref[j, 0] = xr
        st_ref[j, 1] = xi
        if emit_y:
            y_ref[0, :, j * MXU_DIM:(j + 1) * MXU_DIM] = _dot(bu_ref[...].astype(BF16), ct_ref[j])

    @pl.when(c == pl.num_programs(1) - 1)
    def _():
        fin_ref[0] = st_ref[...]


def _s5_scan(xp, g, bt, ct, lr, li, init, tt, reverse, emit_y):
    nrg, rows, d = xp.shape
    s = rows // SUBLANES
    nc = s // tt
    ntile, _, sw = bt.shape
    half = sw // 2
    blk = tt * SUBLANES
    if reverse:
        xmap = lambda gi, c: (gi, nc - 1 - c, 0)
    else:
        xmap = lambda gi, c: (gi, c, 0)
    st_shape = (ntile, 2, SUBLANES, half)
    fin_spec = pl.BlockSpec((1,) + st_shape, lambda gi, c: (gi, 0, 0, 0, 0))
    fin_shape = jax.ShapeDtypeStruct((nrg,) + st_shape, F32)
    if emit_y:
        out_specs = [pl.BlockSpec((1, blk, d), xmap), fin_spec]
        out_shape = [jax.ShapeDtypeStruct(xp.shape, F32), fin_shape]
    else:
        out_specs = [fin_spec]
        out_shape = [fin_shape]
    outs = pl.pallas_call(
        functools.partial(_s5_scan_kernel, tt=tt, reverse=reverse, emit_y=emit_y), grid=(nrg, nc),
        in_specs=[pl.BlockSpec((1, blk, d), xmap), _resident(g.shape), _resident(bt.shape),
                  _resident(ct.shape), _resident(lr.shape), _resident(li.shape),
                  pl.BlockSpec((1,) + st_shape, lambda gi, c: (gi, 0, 0, 0, 0))],
        out_specs=out_specs, out_shape=out_shape,
        scratch_shapes=[pltpu.VMEM((blk, sw), F32), pltpu.VMEM(st_shape, F32)],
        compiler_params=_cparams("parallel", "arbitrary"),
        name="s5_scan_" + ("bwd" if reverse else "fwd") + ("" if emit_y else "_state"),
    )(xp, g, bt, ct, lr, li, init)
    if emit_y:
        return outs[0], outs[1]
    return None, outs[0]


def _topk2(logits):
    lane = lax.broadcasted_iota(jnp.int32, logits.shape, 1)
    lg = jnp.where(lane < N_EXPERTS, logits, -jnp.inf)
    m1 = jnp.max(lg, axis=-1, keepdims=True)
    i1 = jnp.min(jnp.where(lg == m1, lane, LANES), axis=-1, keepdims=True)
    lg2 = jnp.where(lane == i1, -jnp.inf, lg)
    m2 = jnp.max(lg2, axis=-1, keepdims=True)
    i2 = jnp.min(jnp.where(lg2 == m2, lane, LANES), axis=-1, keepdims=True)
    e = jnp.exp(m2 - m1)
    g1 = 1.0 / (1.0 + e)
    g2 = e / (1.0 + e)
    idx = jnp.where(lane == 0, i1, jnp.where(lane == 1, i2, 0))
    gate = jnp.where(lane == 0, g1, jnp.where(lane == 1, g2, 0.0))
    return idx, gate


def _s5_glu_router_kernel(x_ref, yf_ref, yb_ref, gm_ref, d_ref, wglu_ref, gf_ref, wr_ref,
                          x1_ref, h2_ref, idx_ref, gate_ref):
    x = x_ref[...]
    d = x.shape[1]
    h = _rms(x, gm_ref[...])
    y = yf_ref[...] + yb_ref[...] + d_ref[...] * h
    g = jax.nn.gelu(y, approximate=True).astype(BF16)
    z = _dot(g, wglu_ref[...])
    x1 = x + z[:, :d] * jax.nn.sigmoid(z[:, d:])
    x1_ref[...] = x1
    h2 = _rms(x1, gf_ref[...])
    h2_ref[...] = h2
    logits = jnp.dot(h2, wr_ref[...], preferred_element_type=F32, precision=lax.Precision.HIGHEST)
    idx, gate = _topk2(logits)
    idx_ref[...] = idx
    gate_ref[...] = gate


def _s5_glu_router(x, yf, yb, gm, dskip, wglu, gf, wr, tm):
    t, d = x.shape
    row = pl.BlockSpec((tm, d), lambda i: (i, 0))
    nar = pl.BlockSpec((tm, LANES), lambda i: (i, 0))
    return pl.pallas_call(
        _s5_glu_router_kernel, grid=(t // tm,),
        in_specs=[row, row, row, _resident(gm.shape), _resident(dskip.shape), _resident(wglu.shape),
                  _resident(gf.shape), _resident(wr.shape)],
        out_specs=[row, row, nar, nar],
        out_shape=[jax.ShapeDtypeStruct((t, d), F32), jax.ShapeDtypeStruct((t, d), F32),
                   jax.ShapeDtypeStruct((t, LANES), jnp.int32), jax.ShapeDtypeStruct((t, LANES), F32)],
        compiler_params=_cparams("parallel"), name="s5_glu_router",
    )(x, yf, yb, gm, dskip, wglu, gf, wr)


def _moe_plan_kernel(idx_ref, ltri_ref, rank_ref, cnt_ref, carry_ref):
    @pl.when(pl.program_id(0) == 0)
    def _():
        carry_ref[...] = jnp.zeros_like(carry_ref)

    idx = idx_ref[...]
    lane = lax.broadcasted_iota(jnp.int32, idx.shape, 1)
    oh1 = lane == idx[:, 0:1]
    oh2 = lane == idx[:, 1:2]
    oh = jnp.where(oh1, 1.0, jnp.where(oh2, 1.0, 0.0)).astype(BF16)
    cs = _dot(ltri_ref[...], oh)
    tot = carry_ref[...] + cs - 1.0
    r1 = jnp.sum(jnp.where(oh1, tot, 0.0), axis=-1, keepdims=True)
    r2 = jnp.sum(jnp.where(oh2, tot, 0.0), axis=-1, keepdims=True)
    rank_ref[...] = jnp.where(lane == 0, r1, jnp.where(lane == 1, r2, 0.0)).astype(jnp.int32)
    new = carry_ref[...] + cs[cs.shape[0] - 1:, :]
    carry_ref[...] = new
    cnt_ref[...] = new


def _moe_plan(idx, tm):
    t = idx.shape[0]
    ltri = jnp.tril(jnp.ones((tm, tm), F32)).astype(BF16)
    nar = pl.BlockSpec((tm, LANES), lambda i: (i, 0))
    return pl.pallas_call(
        _moe_plan_kernel, grid=(t // tm,),
        in_specs=[nar, _resident(ltri.shape)],
        out_specs=[nar, pl.BlockSpec((1, LANES), lambda i: (0, 0))],
        out_shape=[jax.ShapeDtypeStruct((t, LANES), jnp.int32), jax.ShapeDtypeStruct((1, LANES), F32)],
        scratch_shapes=[pltpu.VMEM((1, LANES), F32)],
        compiler_params=_cparams("arbitrary"), name="moe_plan",
    )(idx, ltri)


def _row_copy(src, dst, i, j, sem):
    return pltpu.make_async_copy(src.at[pl.ds(i, 1)], dst.at[pl.ds(j, 1)], sem)


def _moe_dispatch_kernel(pos_ref, x_hbm, xs_in, xs_hbm, sem, *, tm):
    del xs_in
    base = pl.program_id(0) * tm

    def issue(r, c):
        _row_copy(x_hbm, xs_hbm, base + r, pos_ref[0, 0, 2 * r], sem).start()
        _row_copy(x_hbm, xs_hbm, base + r, pos_ref[0, 0, 2 * r + 1], sem).start()
        return c

    lax.fori_loop(0, tm, issue, 0)

    def drain(r, c):
        _row_copy(x_hbm, xs_hbm, 0, 0, sem).wait()
        _row_copy(x_hbm, xs_hbm, 0, 0, sem).wait()
        return c

    lax.fori_loop(0, tm, drain, 0)


def _moe_dispatch(h2, pos, rows, tm):
    t, d = h2.shape
    pos3 = pos.reshape(t // tm, 1, 2 * tm)
    xs0 = jnp.zeros((rows, d), F32)
    return pl.pallas_call(
        functools.partial(_moe_dispatch_kernel, tm=tm), grid=(t // tm,),
        in_specs=[pl.BlockSpec((1, 1, 2 * tm), lambda i: (i, 0, 0), memory_space=pltpu.SMEM),
                  pl.BlockSpec(memory_space=pl.ANY), pl.BlockSpec(memory_space=pl.ANY)],
        out_specs=pl.BlockSpec(memory_space=pl.ANY),
        out_shape=jax.ShapeDtypeStruct((rows, d), F32),
        scratch_shapes=[pltpu.SemaphoreType.DMA(())],
        input_output_aliases={2: 0},
        compiler_params=_cparams("arbitrary"), name="moe_dispatch",
    )(pos3, h2, xs0)


def _moe_expert_kernel(te_ref, nu_ref, xs_ref, wg_ref, wu_ref, wd_ref, y_ref, act_ref, *, fc):
    i = pl.program_id(0)

    @pl.when(i < nu_ref[0])
    def _():
        y_ref[...] = _swiglu(xs_ref[...].astype(BF16), wg_ref.at[0], wu_ref.at[0], wd_ref.at[0], act_ref, fc)

    @pl.when(i >= nu_ref[0])
    def _():
        y_ref[...] = jnp.zeros_like(y_ref)


def _moe_experts(xs, tile_expert, n_used, wg, wu, wd, tm):
    rows, d = xs.shape
    f = wg.shape[2]
    fc = _tile(f, 1792, LANES)
    wspec = lambda shp: pl.BlockSpec((1,) + shp, lambda i, te, nu: (te[i], 0, 0), pipeline_mode=pl.Buffered(1))
    gs = pltpu.PrefetchScalarGridSpec(
        num_scalar_prefetch=2, grid=(rows // tm,),
        in_specs=[pl.BlockSpec((tm, d), lambda i, te, nu: (i, 0)),
                  wspec((d, f)), wspec((d, f)), wspec((f, d))],
        out_specs=pl.BlockSpec((tm, d), lambda i, te, nu: (i, 0)),
        scratch_shapes=[pltpu.VMEM((tm, f), BF16)])
    return pl.pallas_call(
        functools.partial(_moe_expert_kernel, fc=fc), grid_spec=gs,
        out_shape=jax.ShapeDtypeStruct((rows, d), F32),
        compiler_params=_cparams("arbitrary"), name="moe_experts",
    )(tile_expert, n_used, xs, wg, wu, wd)


def _moe_combine_kernel(pos_ref, x_ref, gate_ref, ys_hbm, gf_ref, o_ref, b0_ref, b1_ref, sem, *, tm, final_norm):
    def issue(r, c):
        _row_copy(ys_hbm, b0_ref, pos_ref[0, 0, 2 * r], r, sem).start()
        _row_copy(ys_hbm, b1_ref, pos_ref[0, 0, 2 * r + 1], r, sem).start()
        return c

    lax.fori_loop(0, tm, issue, 0)

    def drain(r, c):
        _row_copy(ys_hbm, b0_ref, 0, 0, sem).wait()
        _row_copy(ys_hbm, b1_ref, 0, 0, sem).wait()
        return c

    lax.fori_loop(0, tm, drain, 0)
    gate = gate_ref[...]
    out = x_ref[...] + gate[:, 0:1] * b0_ref[...] + gate[:, 1:2] * b1_ref[...]
    if final_norm:
        out = _rms(out, gf_ref[...])
    o_ref[...] = out


def _moe_combine(x1, gate, ys, pos, gfin, tm, final_norm):
    t, d = x1.shape
    pos3 = pos.reshape(t // tm, 1, 2 * tm)
    return pl.pallas_call(
        functools.partial(_moe_combine_kernel, tm=tm, final_norm=final_norm), grid=(t // tm,),
        in_specs=[pl.BlockSpec((1, 1, 2 * tm), lambda i: (i, 0, 0), memory_space=pltpu.SMEM),
                  pl.BlockSpec((tm, d), lambda i: (i, 0)),
                  pl.BlockSpec((tm, LANES), lambda i: (i, 0)),
                  pl.BlockSpec(memory_space=pl.ANY), _resident(gfin.shape)],
        out_specs=pl.BlockSpec((tm, d), lambda i: (i, 0)),
        out_shape=jax.ShapeDtypeStruct((t, d), F32),
        scratch_shapes=[pltpu.VMEM((tm, d), F32), pltpu.VMEM((tm, d), F32), pltpu.SemaphoreType.DMA(())],
        compiler_params=_cparams("arbitrary"), name="moe_combine",
    )(pos3, x1, gate, ys, gfin)


def _rmsnorm_kernel(x_ref, g_ref, o_ref):
    o_ref[...] = _rms(x_ref[...], g_ref[...])


def _rmsnorm(x, g, tm):
    t, d = x.shape
    return pl.pallas_call(
        _rmsnorm_kernel, grid=(t // tm,),
        in_specs=[pl.BlockSpec((tm, d), lambda i: (i, 0)), _resident(g.shape)],
        out_specs=pl.BlockSpec((tm, d), lambda i: (i, 0)),
        out_shape=jax.ShapeDtypeStruct((t, d), F32),
        compiler_params=_cparams("parallel"), name="final_norm",
    )(x, g)


def _rot_cols(w):
    half = QK_ROPE // 2
    return jnp.concatenate([-w[..., half:], w[..., :half]], axis=-1)


def _prep_mla(w_dq, q_norm, w_uq, w_dkv, kv_norm, w_ukv, w_o):
    ql = w_dq.shape[1]
    kvl = w_ukv.shape[0]
    wq = w_uq.reshape(ql, N_HEADS, QK_NOPE + QK_ROPE)
    wq_nope, wq_rope = wq[..., :QK_NOPE], wq[..., QK_NOPE:]
    zpad = jnp.zeros((ql, N_HEADS, HEAD_PAD - QK_NOPE - QK_ROPE), F32)
    w1 = jnp.concatenate([wq_nope, wq_rope, zpad], axis=-1).reshape(ql, N_HEADS * HEAD_PAD)
    w2 = jnp.concatenate([jnp.zeros_like(wq_nope), _rot_cols(wq_rope), zpad], axis=-1).reshape(ql, N_HEADS * HEAD_PAD)
    kr_w = w_dkv[:, kvl:]
    lpad = jnp.zeros((w_dkv.shape[0], LANES - QK_ROPE), F32)
    wdkv = jnp.concatenate([w_dkv[:, :kvl], kr_w, lpad, _rot_cols(kr_w), lpad], axis=1)
    wkv = w_ukv.reshape(kvl, N_HEADS, QK_NOPE + V_DIM)
    wk_top = jnp.concatenate([wkv[..., :QK_NOPE], jnp.zeros((kvl, N_HEADS, HEAD_PAD - QK_NOPE), F32)], axis=-1)
    eye = jnp.eye(LANES, QK_ROPE, dtype=F32)
    wk_bot = jnp.concatenate([jnp.zeros((LANES, QK_NOPE), F32), eye,
                              jnp.zeros((LANES, HEAD_PAD - QK_NOPE - QK_ROPE), F32)], axis=-1)
    wk_bot = jnp.broadcast_to(wk_bot[:, None, :], (LANES, N_HEADS, HEAD_PAD))
    wk = jnp.concatenate([wk_top, wk_bot], axis=0).reshape(kvl + LANES, N_HEADS * HEAD_PAD)
    wvt = wkv[..., QK_NOPE:].reshape(kvl, N_HEADS * V_DIM).T
    return dict(wdq=w_dq.astype(BF16), qn=q_norm.reshape(1, -1), w1t=w1.T.astype(BF16), w2t=w2.T.astype(BF16),
                wdkv=wdkv.astype(BF16), kvn=kv_norm.reshape(1, -1), wk=wk.astype(BF16), wvt=wvt.astype(BF16),
                wo=w_o.astype(BF16))


def _rope_tables(s):
    pos = jnp.arange(s, dtype=F32)
    inv = ROPE_THETA ** (-jnp.arange(0, QK_ROPE, 2, dtype=F32) / QK_ROPE)
    ang = pos[:, None] * inv[None, :]
    ang = jnp.concatenate([ang, ang], axis=-1)
    cos, sin = jnp.cos(ang), jnp.sin(ang)
    qs = ATTN_SCALE * LOG2E
    zq = jnp.zeros((s, HEAD_PAD - QK_NOPE - QK_ROPE), F32)
    ctq = jnp.concatenate([jnp.full((s, QK_NOPE), qs, F32), cos * qs, zq], axis=1).T
    stq = jnp.concatenate([jnp.zeros((s, QK_NOPE), F32), sin * qs, zq], axis=1).T
    zk = jnp.zeros((s, LANES - QK_ROPE), F32)
    ctk = jnp.concatenate([cos, zk], axis=1)
    stk = jnp.concatenate([sin, zk], axis=1)
    return ctq, stq, ctk, stk


def _prep_s5(a_re, a_im, log_step, b_re, b_im, c_re, c_im):
    _, g, p = a_re.shape
    n = g * p * GROUP_CH
    rep = lambda a: jnp.broadcast_to(a[..., None], (2, g, p, GROUP_CH)).reshape(2, n)
    ls = jnp.broadcast_to(log_step[:, :, None, None], (2, g, p, GROUP_CH)).reshape(2, n)
    outs = pl.pallas_call(
        _s5_param_kernel,
        out_shape=[jax.ShapeDtypeStruct((2, n), F32)] * 4, name="s5_params",
    )(rep(a_re), rep(a_im), ls, b_re.reshape(2, n), b_im.reshape(2, n))
    lr, li, bbr, bbi = [o.reshape(2, g, p, GROUP_CH) for o in outs]
    lr, li = lr[..., 0], li[..., 0]
    gpt = MXU_DIM // GROUP_CH
    ntile = g // gpt
    eye = jnp.eye(gpt, dtype=F32)

    def tiles_b(bb):
        t = bb.reshape(2, ntile, gpt, p, GROUP_CH)
        return jnp.einsum('djapc,ab->djacbp', t, eye).reshape(2, ntile, gpt * GROUP_CH, gpt * p)

    def tiles_c(cc):
        t = cc.reshape(2, ntile, gpt, GROUP_CH, p)
        return jnp.einsum('djacp,ab->djapbc', t, eye).reshape(2, ntile, gpt * p, gpt * GROUP_CH)

    bt = jnp.concatenate([tiles_b(bbr), tiles_b(bbi)], axis=-1).astype(BF16)
    ct = jnp.concatenate([tiles_c(c_re), tiles_c(-c_im)], axis=-2).astype(BF16)
    lrt = lr.reshape(2, ntile, 1, gpt * p)
    lit = li.reshape(2, ntile, 1, gpt * p)
    return bt, ct, lrt, lit


def _cpow(lr, li, n):
    k = int(math.log2(n))
    assert 2 ** k == n
    for _ in range(k):
        lr, li = lr * lr - li * li, 2.0 * lr * li
    return lr, li


def _mla_layer(x, p, j, tabs, tm, tq, tk):
    w = p['mla'][j]
    qt, k, vt = _mla_pre(x, p['norm_mix'][2 * j], w, tabs, tm)
    ot = _attention(qt, k, vt, tq, tk)
    f = p['ffn'][j]
    return _attn_out_ffn(x, ot, w['wo'], p['norm_ffn'][2 * j], f['wg'], f['wu'], f['wd'], tm)


def _s5_states(xp, g, s5, nseg, tt):
    bt, ct, lr, li = s5
    nrg = xp.shape[0]
    ntile, half = bt.shape[1], bt.shape[3] // 2
    zero = jnp.zeros((nrg, ntile, 2, SUBLANES, half), F32)
    if nseg == 1:
        return zero, zero
    seg_len = xp.shape[1] // SUBLANES
    inits = []
    for k, reverse in ((0, False), (1, True)):
        _, fin = _s5_scan(xp, g, bt[k], ct[k], lr[k], li[k], zero, tt, reverse, False)
        e = fin.reshape(nrg, ntile, 2, SUBLANES // nseg, nseg, half)
        plr, pli = _cpow(lr[k], li[k], seg_len)
        plr, pli = plr[None], pli[None]
        order = range(nseg - 1, -1, -1) if reverse else range(nseg)
        sr = jnp.zeros_like(e[:, :, 0, :, 0])
        si = jnp.zeros_like(sr)
        init_r = [None] * nseg
        init_i = [None] * nseg
        for q in order:
            init_r[q], init_i[q] = sr, si
            er, ei = e[:, :, 0, :, q], e[:, :, 1, :, q]
            sr, si = plr * sr - pli * si + er, plr * si + pli * sr + ei
        init = jnp.stack([jnp.stack(init_r, axis=3), jnp.stack(init_i, axis=3)], axis=2)
        inits.append(init.reshape(nrg, ntile, 2, SUBLANES, half))
    return inits[0], inits[1]


def _s5_moe_layer(xp, p, j, nseg, tt, tm, last):
    nrg, rows, d = xp.shape
    i = 2 * j + 1
    s5 = p['s5'][j]
    bt, ct, lr, li = s5
    gm = p['norm_mix'][i]
    init_f, init_b = _s5_states(xp, gm, s5, nseg, tt)
    yf, _ = _s5_scan(xp, gm, bt[0], ct[0], lr[0], li[0], init_f, tt, False, True)
    yb, _ = _s5_scan(xp, gm, bt[1], ct[1], lr[1], li[1], init_b, tt, True, True)
    t = nrg * rows
    m = p['moe'][j]
    x1, h2, idx, gate = _s5_glu_router(xp.reshape(t, d), yf.reshape(t, d), yb.reshape(t, d), gm,
                                       p['s5_d'][j], p['s5_wglu'][j], p['norm_ffn'][i], m['wr'], tm)
    rank, cnt = _moe_plan(idx, tm)
    counts = cnt[0, :N_EXPERTS].astype(jnp.int32)
    ntiles = (counts + tm - 1) // tm
    tile_end = jnp.cumsum(ntiles)
    offs = (tile_end - ntiles) * tm
    e2 = idx[:, :2]
    pos = offs[e2] + rank[:, :2]
    nt_max = (2 * t) // tm + N_EXPERTS
    tile_expert = jnp.minimum(jnp.searchsorted(tile_end, jnp.arange(nt_max, dtype=jnp.int32), side='right'),
                              N_EXPERTS - 1).astype(jnp.int32)
    n_used = tile_end[-1:].astype(jnp.int32)
    xs = _moe_dispatch(h2, pos, nt_max * tm, tm)
    ys = _moe_experts(xs, tile_expert, n_used, m['wg'], m['wu'], m['wd'], tm)
    out = _moe_combine(x1, gate, ys, pos, p['norm_final'], tm, last)
    return out.reshape(nrg, rows, d)


def _to_perm(x, nseg):
    b, s, d = x.shape
    nrg = b * nseg // SUBLANES
    xr = x.reshape(nrg, SUBLANES, s // nseg, d)
    return jnp.swapaxes(xr, 1, 2).reshape(nrg, (s // nseg) * SUBLANES, d)


def _from_perm(xp, b, s, nseg):
    nrg, rows, d = xp.shape
    xr = xp.reshape(nrg, rows // SUBLANES, SUBLANES, d)
    return jnp.swapaxes(xr, 1, 2).reshape(b, s, d)


def _trunk(x, p):
    b, s, d = x.shape
    tm = _tile(s, 512)
    tq = _tile(s, 512, LANES)
    tk = _tile(s, 512, LANES)
    nseg = 1 if b % SUBLANES == 0 else SUBLANES // b
    assert (b * nseg) % SUBLANES == 0 and s % nseg == 0
    tt = _tile(s // nseg, 64)
    tabs = _rope_tables(s)
    depth = p['norm_mix'].shape[0]
    for i in range(depth):
        j = i // 2
        if i % 2 == 0:
            x = _mla_layer(x, p, j, tabs, tm, tq, tk)
        else:
            xp = _s5_moe_layer(_to_perm(x, nseg), p, j, nseg, tt, tm, last=(i == depth - 1))
            x = _from_perm(xp, b, s, nseg)
    if depth % 2 == 1:
        x = _rmsnorm(x.reshape(b * s, d), p['norm_final'], tm).reshape(b, s, d)
    return x


def kernel(x_prompt, x_sample, norm_mix, norm_ffn, norm_final, mla_w_dq, mla_q_norm, mla_w_uq, mla_w_dkv,
           mla_kv_norm, mla_w_ukv, mla_w_o, ssm_a_re, ssm_a_im, ssm_log_step, ssm_b_re, ssm_b_im, ssm_c_re,
           ssm_c_im, ssm_d, ssm_w_glu, ffn_w_gate, ffn_w_up, ffn_w_down, moe_w_router, moe_w_gate, moe_w_up,
           moe_w_down):
    d = x_prompt.shape[-1]
    na, ns = mla_w_dq.shape[0], ssm_a_re.shape[0]
    p = dict(
        norm_mix=norm_mix.reshape(-1, 1, d), norm_ffn=norm_ffn.reshape(-1, 1, d), norm_final=norm_final.reshape(1, d),
        mla=[_prep_mla(mla_w_dq[j], mla_q_norm[j], mla_w_uq[j], mla_w_dkv[j], mla_kv_norm[j], mla_w_ukv[j],
                       mla_w_o[j]) for j in range(na)],
        ffn=[dict(wg=ffn_w_gate[j].astype(BF16), wu=ffn_w_up[j].astype(BF16), wd=ffn_w_down[j].astype(BF16))
             for j in range(na)],
        s5=[_prep_s5(ssm_a_re[j], ssm_a_im[j], ssm_log_step[j], ssm_b_re[j], ssm_b_im[j], ssm_c_re[j], ssm_c_im[j])
            for j in range(ns)],
        s5_d=[ssm_d[j].reshape(1, d) for j in range(ns)],
        s5_wglu=[ssm_w_glu[j].astype(BF16) for j in range(ns)],
        moe=[dict(wr=jnp.pad(moe_w_router[j], ((0, 0), (0, LANES - N_EXPERTS))),
                  wg=moe_w_gate[j].astype(BF16), wu=moe_w_up[j].astype(BF16), wd=moe_w_down[j].astype(BF16))
             for j in range(ns)],
    )
    return (_trunk(x_prompt, p), _trunk(x_sample, p))
```

```python
import functools
import math

import jax
import jax.numpy as jnp
from jax import lax
from jax.experimental import pallas as pl
from jax.experimental.pallas import tpu as pltpu

BF16 = jnp.bfloat16
F32 = jnp.float32

N_HEADS = 16
QK_NOPE = 64
QK_ROPE = 32
V_DIM = 64
HEAD_PAD = 128
ROPE_THETA = 10000.0
ATTN_SCALE = (QK_NOPE + QK_ROPE) ** -0.5
LOG2E = 1.4426950408889634
GROUP_CH = 16
STATE_DIM = 64
N_EXPERTS = 8
NORM_EPS = 1e-6
LANES = 128
SUBLANES = 8
MXU_DIM = 256
VMEM_LIMIT = 56 * 1024 * 1024
NEG_BIG = -1e30


def _cparams(*sem):
    return pltpu.CompilerParams(dimension_semantics=sem, vmem_limit_bytes=VMEM_LIMIT)


def _resident(shape):
    zeros = (0,) * len(shape)
    return pl.BlockSpec(shape, lambda *_: zeros, pipeline_mode=pl.Buffered(1))


def _tile(n, cap, mult=SUBLANES):
    if n <= cap:
        return n
    t = (cap // mult) * mult
    while t > mult and n % t:
        t -= mult
    assert n % t == 0, (n, cap, mult)
    return t


def _rms(x, g):
    ms = jnp.mean(x * x, axis=-1, keepdims=True)
    return x * lax.rsqrt(ms + NORM_EPS) * g


def _dot(a, b):
    return jnp.dot(a, b, preferred_element_type=F32)


def _dot_nt(a, b):
    return lax.dot_general(a, b, (((1,), (1,)), ((), ())), preferred_element_type=F32)


def _dot_tn(a, b):
    return lax.dot_general(a, b, (((0,), (0,)), ((), ())), preferred_element_type=F32)


def _swiglu(h, wg_ref, wu_ref, wd_ref, act_ref, fc):
    f = act_ref.shape[1]
    for c0 in range(0, f, fc):
        g = _dot(h, wg_ref[:, c0:c0 + fc])
        u = _dot(h, wu_ref[:, c0:c0 + fc])
        act_ref[:, c0:c0 + fc] = (g * jax.nn.sigmoid(g) * u).astype(BF16)
    return _dot(act_ref[...], wd_ref[...])


def _mla_pre_kernel(x_ref, g_ref, wdq_ref, qn_ref, w1_ref, w2_ref, wdkv_ref, kvn_ref, wk_ref, wv_ref,
                    ctq_ref, stq_ref, ctk_ref, stk_ref, qt_ref, k_ref, vt_ref):
    h = _rms(x_ref[0], g_ref[...]).astype(BF16)
    cq = _rms(_dot(h, wdq_ref[...]), qn_ref[...]).astype(BF16)
    a = _dot_nt(w1_ref[...], cq)
    b = _dot_nt(w2_ref[...], cq)
    ct = ctq_ref[...]
    st = stq_ref[...]
    for hh in range(N_HEADS):
        sl = slice(hh * HEAD_PAD, (hh + 1) * HEAD_PAD)
        qt_ref[0, sl, :] = (a[sl] * ct + b[sl] * st).astype(BF16)
    ck = _dot(h, wdkv_ref[...])
    kvl = kvn_ref.shape[1]
    ckv = _rms(ck[:, :kvl], kvn_ref[...]).astype(BF16)
    kr = (ck[:, kvl:kvl + LANES] * ctk_ref[...] + ck[:, kvl + LANES:] * stk_ref[...]).astype(BF16)
    kaug = jnp.concatenate([ckv, kr], axis=1)
    k_ref[0] = _dot(kaug, wk_ref[...]).astype(BF16)
    vt_ref[0] = _dot_nt(wv_ref[...], ckv).astype(BF16)


def _mla_pre(x, g, w, tabs, tm):
    b, s, d = x.shape
    hq = N_HEADS * HEAD_PAD
    hv = N_HEADS * V_DIM
    ctq, stq, ctk, stk = tabs
    grid = (b, s // tm)
    in_specs = [
        pl.BlockSpec((1, tm, d), lambda bi, i: (bi, i, 0)),
        _resident(g.shape), _resident(w['wdq'].shape), _resident(w['qn'].shape),
        _resident(w['w1t'].shape), _resident(w['w2t'].shape), _resident(w['wdkv'].shape),
        _resident(w['kvn'].shape), _resident(w['wk'].shape), _resident(w['wvt'].shape),
        pl.BlockSpec((HEAD_PAD, tm), lambda bi, i: (0, i)),
        pl.BlockSpec((HEAD_PAD, tm), lambda bi, i: (0, i)),
        pl.BlockSpec((tm, LANES), lambda bi, i: (i, 0)),
        pl.BlockSpec((tm, LANES), lambda bi, i: (i, 0)),
    ]
    out_specs = [
        pl.BlockSpec((1, hq, tm), lambda bi, i: (bi, 0, i)),
        pl.BlockSpec((1, tm, hq), lambda bi, i: (bi, i, 0)),
        pl.BlockSpec((1, hv, tm), lambda bi, i: (bi, 0, i)),
    ]
    out_shape = [jax.ShapeDtypeStruct((b, hq, s), BF16), jax.ShapeDtypeStruct((b, s, hq), BF16),
                 jax.ShapeDtypeStruct((b, hv, s), BF16)]
    return pl.pallas_call(
        _mla_pre_kernel, grid=grid, in_specs=in_specs, out_specs=out_specs, out_shape=out_shape,
        compiler_params=_cparams("parallel", "parallel"), name="mla_pre",
    )(x, g, w['wdq'], w['qn'], w['w1t'], w['w2t'], w['wdkv'], w['kvn'], w['wk'], w['wvt'], ctq, stq, ctk, stk)


SUM_ROWS = 16


def _attn_kernel(qt_ref, k_ref, vt_ref, o_ref, sa_ref, sb_ref, acc_ref, m_ref, *, tk):
    qt = qt_ref[0]
    nk = k_ref.shape[1] // tk
    ones = jnp.ones((SUM_ROWS, tk), BF16)

    def scores(j, dst_ref):
        k0 = pl.multiple_of(j * tk, tk)
        dst_ref[...] = _dot(k_ref[0, pl.ds(k0, tk), :], qt)

    def update(j, src_ref):
        k0 = pl.multiple_of(j * tk, tk)
        s = src_ref[...]
        m_old = m_ref[...]
        m_new = jnp.maximum(m_old, jnp.max(s, axis=0, keepdims=True))
        p = jnp.exp2(s - m_new).astype(BF16)
        v = jnp.concatenate([vt_ref[0, :, pl.ds(k0, tk)], ones], axis=0)
        acc_ref[...] = jnp.exp2(m_old - m_new) * acc_ref[...] + _dot(v, p)
        m_ref[...] = m_new

    m_ref[...] = jnp.full(m_ref.shape, NEG_BIG, F32)
    acc_ref[...] = jnp.zeros(acc_ref.shape, F32)
    scores(0, sa_ref)

    def pair(i, c):
        j = 2 * i
        scores(j + 1, sb_ref)
        update(j, sa_ref)
        scores(j + 2, sa_ref)
        update(j + 1, sb_ref)
        return c

    lax.fori_loop(0, nk // 2 - 1, pair, 0, unroll=2)
    scores(nk - 1, sb_ref)
    update(nk - 2, sa_ref)
    update(nk - 1, sb_ref)
    acc = acc_ref[...]
    o_ref[0] = (acc[:V_DIM] / acc[V_DIM:V_DIM + 1]).astype(BF16)


def _attention(qt, k, vt, tq, tk):
    b, hq, s = qt.shape
    assert (s // tk) % 2 == 0
    grid = (b, N_HEADS, s // tq)
    return pl.pallas_call(
        functools.partial(_attn_kernel, tk=tk), grid=grid,
        in_specs=[pl.BlockSpec((1, HEAD_PAD, tq), lambda bi, h, i: (bi, h, i)),
                  pl.BlockSpec((1, s, HEAD_PAD), lambda bi, h, i: (bi, 0, h)),
                  pl.BlockSpec((1, V_DIM, s), lambda bi, h, i: (bi, h, 0))],
        out_specs=pl.BlockSpec((1, V_DIM, tq), lambda bi, h, i: (bi, h, i)),
        out_shape=jax.ShapeDtypeStruct((b, N_HEADS * V_DIM, s), BF16),
        scratch_shapes=[pltpu.VMEM((tk, tq), F32), pltpu.VMEM((tk, tq), F32),
                        pltpu.VMEM((V_DIM + SUM_ROWS, tq), F32), pltpu.VMEM((1, tq), F32)],
        compiler_params=_cparams("parallel", "parallel", "parallel"), name="mla_attention",
    )(qt, k, vt)


def _attn_out_ffn_kernel(x_ref, ot_ref, wo_ref, g_ref, wg_ref, wu_ref, wd_ref, y_ref, act_ref, *, fc):
    x1 = x_ref[0] + _dot_tn(ot_ref[0], wo_ref[...])
    h = _rms(x1, g_ref[...]).astype(BF16)
    y_ref[0] = x1 + _swiglu(h, wg_ref, wu_ref, wd_ref, act_ref, fc)


def _attn_out_ffn(x, ot, wo, g, wg, wu, wd, tm):
    b, s, d = x.shape
    f = wg.shape[1]
    fc = _tile(f, 1536, LANES)
    hv = ot.shape[1]
    return pl.pallas_call(
        functools.partial(_attn_out_ffn_kernel, fc=fc), grid=(b, s // tm),
        in_specs=[pl.BlockSpec((1, tm, d), lambda bi, i: (bi, i, 0)),
                  pl.BlockSpec((1, hv, tm), lambda bi, i: (bi, 0, i)),
                  _resident(wo.shape), _resident(g.shape), _resident(wg.shape), _resident(wu.shape),
                  _resident(wd.shape)],
        out_specs=pl.BlockSpec((1, tm, d), lambda bi, i: (bi, i, 0)),
        out_shape=jax.ShapeDtypeStruct((b, s, d), F32),
        scratch_shapes=[pltpu.VMEM((tm, f), BF16)],
        compiler_params=_cparams("parallel", "parallel"), name="attn_out_ffn",
    )(x, ot, wo, g, wg, wu, wd)


def _s5_param_kernel(are_ref, aim_ref, ls_ref, bre_ref, bim_ref, lr_ref, li_ref, bbr_ref, bbi_ref):
    ar = are_ref[...]
    ai = aim_ref[...]
    delta = jnp.exp(ls_ref[...])
    mag = jnp.exp(ar * delta)
    lr = mag * jnp.cos(ai * delta)
    li = mag * jnp.sin(ai * delta)
    nr = lr - 1.0
    den = ar * ar + ai * ai
    cr = (nr * ar + li * ai) / den
    ci = (li * ar - nr * ai) / den
    br = bre_ref[...]
    bi = bim_ref[...]
    lr_ref[...] = lr
    li_ref[...] = li
    bbr_ref[...] = cr * br - ci * bi
    bbi_ref[...] = cr * bi + ci * br


def _s5_scan_kernel(x_ref, g_ref, bt_ref, ct_ref, lr_ref, li_ref, init_ref, *rest, tt, reverse, emit_y):
    if emit_y:
        y_ref, fin_ref, bu_ref, st_ref = rest
    else:
        fin_ref, bu_ref, st_ref = rest
    c = pl.program_id(1)
    ntile = bt_ref.shape[0]
    half = bt_ref.shape[2] // 2

    @pl.when(c == 0)
    def _():
        st_ref[...] = init_ref[0]

    hb = _rms(x_ref[0], g_ref[...]).astype(BF16)
    for j in range(ntile):
        bu_ref[...] = _dot(hb[:, j * MXU_DIM:(j + 1) * MXU_DIM], bt_ref[j])
        lr = jnp.broadcast_to(lr_ref[j], (SUBLANES, half))
        li = jnp.broadcast_to(li_ref[j], (SUBLANES, half))

        def step(t, carry, lr=lr, li=li):
            xr, xi = carry
            tau = (tt - 1 - t) if reverse else t
            r0 = pl.multiple_of(tau * SUBLANES, SUBLANES)
            nr = lr * xr - li * xi + bu_ref[pl.ds(r0, SUBLANES), 0:half]
            ni = lr * xi + li * xr + bu_ref[pl.ds(r0, SUBLANES), half:2 * half]
            bu_ref[pl.ds(r0, SUBLANES), 0:half] = nr
            bu_ref[pl.ds(r0, SUBLANES), half:2 * half] = ni
            return nr, ni

        xr, xi = lax.fori_loop(0, tt, step, (st_ref[j, 0], st_ref[j, 1]))
        st_ref[j, 0] = xr
        st_ref[j, 1] = xi
        if emit_y:
            y_ref[0, :, j * MXU_DIM:(j + 1) * MXU_DIM] = _dot(bu_ref[...].astype(BF16), ct_ref[j])

    @pl.when(c == pl.num_programs(1) - 1)
    def _():
        fin_ref[0] = st_ref[...]


def _s5_scan(xp, g, bt, ct, lr, li, init, tt, reverse, emit_y):
    nrg, rows, d = xp.shape
    s = rows // SUBLANES
    nc = s // tt
    ntile, _, sw = bt.shape
    half = sw // 2
    blk = tt * SUBLANES
    if reverse:
        xmap = lambda gi, c: (gi, nc - 1 - c, 0)
    else:
        xmap = lambda gi, c: (gi, c, 0)
    st_shape = (ntile, 2, SUBLANES, half)
    fin_spec = pl.BlockSpec((1,) + st_shape, lambda gi, c: (gi, 0, 0, 0, 0))
    fin_shape = jax.ShapeDtypeStruct((nrg,) + st_shape, F32)
    if emit_y:
        out_specs = [pl.BlockSpec((1, blk, d), xmap), fin_spec]
        out_shape = [jax.ShapeDtypeStruct(xp.shape, F32), fin_shape]
    else:
        out_specs = [fin_spec]
        out_shape = [fin_shape]
    outs = pl.pallas_call(
        functools.partial(_s5_scan_kernel, tt=tt, reverse=reverse, emit_y=emit_y), grid=(nrg, nc),
        in_specs=[pl.BlockSpec((1, blk, d), xmap), _resident(g.shape), _resident(bt.shape),
                  _resident(ct.shape), _resident(lr.shape), _resident(li.shape),
                  pl.BlockSpec((1,) + st_shape, lambda gi, c: (gi, 0, 0, 0, 0))],
        out_specs=out_specs, out_shape=out_shape,
        scratch_shapes=[pltpu.VMEM((blk, sw), F32), pltpu.VMEM(st_shape, F32)],
        compiler_params=_cparams("parallel", "arbitrary"),
        name="s5_scan_" + ("bwd" if reverse else "fwd") + ("" if emit_y else "_state"),
    )(xp, g, bt, ct, lr, li, init)
    if emit_y:
        return outs[0], outs[1]
    return None, outs[0]


def _topk2(logits):
    lane = lax.broadcasted_iota(jnp.int32, logits.shape, 1)
    lg = jnp.where(lane < N_EXPERTS, logits, -jnp.inf)
    m1 = jnp.max(lg, axis=-1, keepdims=True)
    i1 = jnp.min(jnp.where(lg == m1, lane, LANES), axis=-1, keepdims=True)
    lg2 = jnp.where(lane == i1, -jnp.inf, lg)
    m2 = jnp.max(lg2, axis=-1, keepdims=True)
    i2 = jnp.min(jnp.where(lg2 == m2, lane, LANES), axis=-1, keepdims=True)
    e = jnp.exp(m2 - m1)
    g1 = 1.0 / (1.0 + e)
    g2 = e / (1.0 + e)
    idx = jnp.where(lane == 0, i1, jnp.where(lane == 1, i2, 0))
    gate = jnp.where(lane == 0, g1, jnp.where(lane == 1, g2, 0.0))
    return idx, gate


def _s5_glu_router_kernel(x_ref, yf_ref, yb_ref, gm_ref, d_ref, wglu_ref, gf_ref, wr_ref,
                          x1_ref, h2_ref, idx_ref, gate_ref):
    x = x_ref[...]
    d = x.shape[1]
    h = _rms(x, gm_ref[...])
    y = yf_ref[...] + yb_ref[...] + d_ref[...] * h
    g = jax.nn.gelu(y, approximate=True).astype(BF16)
    z = _dot(g, wglu_ref[...])
    x1 = x + z[:, :d] * jax.nn.sigmoid(z[:, d:])
    x1_ref[...] = x1
    h2 = _rms(x1, gf_ref[...])
    h2_ref[...] = h2
    logits = jnp.dot(h2, wr_ref[...], preferred_element_type=F32, precision=lax.Precision.HIGHEST)
    idx, gate = _topk2(logits)
    idx_ref[...] = idx
    gate_ref[...] = gate


def _s5_glu_router(x, yf, yb, gm, dskip, wglu, gf, wr, tm):
    t, d = x.shape
    row = pl.BlockSpec((tm, d), lambda i: (i, 0))
    nar = pl.BlockSpec((tm, LANES), lambda i: (i, 0))
    return pl.pallas_call(
        _s5_glu_router_kernel, grid=(t // tm,),
        in_specs=[row, row, row, _resident(gm.shape), _resident(dskip.shape), _resident(wglu.shape),
                  _resident(gf.shape), _resident(wr.shape)],
        out_specs=[row, row, nar, nar],
        out_shape=[jax.ShapeDtypeStruct((t, d), F32), jax.ShapeDtypeStruct((t, d), F32),
                   jax.ShapeDtypeStruct((t, LANES), jnp.int32), jax.ShapeDtypeStruct((t, LANES), F32)],
        compiler_params=_cparams("parallel"), name="s5_glu_router",
    )(x, yf, yb, gm, dskip, wglu, gf, wr)


def _moe_plan_kernel(idx_ref, ltri_ref, rank_ref, cnt_ref, carry_ref):
    @pl.when(pl.program_id(0) == 0)
    def _():
        carry_ref[...] = jnp.zeros_like(carry_ref)

    idx = idx_ref[...]
    lane = lax.broadcasted_iota(jnp.int32, idx.shape, 1)
    oh1 = lane == idx[:, 0:1]
    oh2 = lane == idx[:, 1:2]
    oh = jnp.where(oh1, 1.0, jnp.where(oh2, 1.0, 0.0)).astype(BF16)
    cs = _dot(ltri_ref[...], oh)
    tot = carry_ref[...] + cs - 1.0
    r1 = jnp.sum(jnp.where(oh1, tot, 0.0), axis=-1, keepdims=True)
    r2 = jnp.sum(jnp.where(oh2, tot, 0.0), axis=-1, keepdims=True)
    rank_ref[...] = jnp.where(lane == 0, r1, jnp.where(lane == 1, r2, 0.0)).astype(jnp.int32)
    new = carry_ref[...] + cs[cs.shape[0] - 1:, :]
    carry_ref[...] = new
    cnt_ref[...] = new


def _moe_plan(idx, tm):
    t = idx.shape[0]
    ltri = jnp.tril(jnp.ones((tm, tm), F32)).astype(BF16)
    nar = pl.BlockSpec((tm, LANES), lambda i: (i, 0))
    return pl.pallas_call(
        _moe_plan_kernel, grid=(t // tm,),
        in_specs=[nar, _resident(ltri.shape)],
        out_specs=[nar, pl.BlockSpec((1, LANES), lambda i: (0, 0))],
        out_shape=[jax.ShapeDtypeStruct((t, LANES), jnp.int32), jax.ShapeDtypeStruct((1, LANES), F32)],
        scratch_shapes=[pltpu.VMEM((1, LANES), F32)],
        compiler_params=_cparams("arbitrary"), name="moe_plan",
    )(idx, ltri)


def _row_copy(src, dst, i, j, sem):
    return pltpu.make_async_copy(src.at[pl.ds(i, 1)], dst.at[pl.ds(j, 1)], sem)


def _moe_dispatch_kernel(pos_ref, x_ref, xs_in, xs_hbm, sem, *, tm):
    del xs_in

    def issue(r, c):
        _row_copy(x_ref, xs_hbm, r, pos_ref[0, 0, 2 * r], sem).start()
        _row_copy(x_ref, xs_hbm, r, pos_ref[0, 0, 2 * r + 1], sem).start()
        return c

    lax.fori_loop(0, tm, issue, 0)

    def drain(r, c):
        _row_copy(x_ref, xs_hbm, 0, 0, sem).wait()
        _row_copy(x_ref, xs_hbm, 0, 0, sem).wait()
        return c

    lax.fori_loop(0, tm, drain, 0)


def _moe_dispatch(h2, pos, rows, tm):
    t, d = h2.shape
    pos3 = pos.reshape(t // tm, 1, 2 * tm)
    xs0 = jnp.zeros((rows, d), F32)
    return pl.pallas_call(
        functools.partial(_moe_dispatch_kernel, tm=tm), grid=(t // tm,),
        in_specs=[pl.BlockSpec((1, 1, 2 * tm), lambda i: (i, 0, 0), memory_space=pltpu.SMEM),
                  pl.BlockSpec((tm, d), lambda i: (i, 0)), pl.BlockSpec(memory_space=pl.ANY)],
        out_specs=pl.BlockSpec(memory_space=pl.ANY),
        out_shape=jax.ShapeDtypeStruct((rows, d), F32),
        scratch_shapes=[pltpu.SemaphoreType.DMA(())],
        input_output_aliases={2: 0},
        compiler_params=_cparams("arbitrary"), name="moe_dispatch",
    )(pos3, h2, xs0)


def _moe_expert_kernel(te_ref, nu_ref, xs_ref, wg_ref, wu_ref, wd_ref, y_ref, act_ref, *, fc):
    i = pl.program_id(0)

    @pl.when(i < nu_ref[0])
    def _():
        y_ref[...] = _swiglu(xs_ref[...].astype(BF16), wg_ref.at[0], wu_ref.at[0], wd_ref.at[0], act_ref, fc)

    @pl.when(i >= nu_ref[0])
    def _():
        y_ref[...] = jnp.zeros_like(y_ref)


def _moe_experts(xs, tile_expert, n_used, wg, wu, wd, tm):
    rows, d = xs.shape
    f = wg.shape[2]
    fc = _tile(f, 1792, LANES)
    wspec = lambda shp: pl.BlockSpec((1,) + shp, lambda i, te, nu: (te[i], 0, 0), pipeline_mode=pl.Buffered(1))
    gs = pltpu.PrefetchScalarGridSpec(
        num_scalar_prefetch=2, grid=(rows // tm,),
        in_specs=[pl.BlockSpec((tm, d), lambda i, te, nu: (i, 0)),
                  wspec((d, f)), wspec((d, f)), wspec((f, d))],
        out_specs=pl.BlockSpec((tm, d), lambda i, te, nu: (i, 0)),
        scratch_shapes=[pltpu.VMEM((tm, f), BF16)])
    return pl.pallas_call(
        functools.partial(_moe_expert_kernel, fc=fc), grid_spec=gs,
        out_shape=jax.ShapeDtypeStruct((rows, d), F32),
        compiler_params=_cparams("arbitrary"), name="moe_experts",
    )(tile_expert, n_used, xs, wg, wu, wd)


def _moe_combine_kernel(pos_ref, x_ref, gate_ref, ys_hbm, gf_ref, o_ref, b0_ref, b1_ref, sem, *, tm, final_norm):
    def issue(r, c):
        _row_copy(ys_hbm, b0_ref, pos_ref[0, 0, 2 * r], r, sem).start()
        _row_copy(ys_hbm, b1_ref, pos_ref[0, 0, 2 * r + 1], r, sem).start()
        return c

    lax.fori_loop(0, tm, issue, 0)

    def drain(r, c):
        _row_copy(ys_hbm, b0_ref, 0, 0, sem).wait()
        _row_copy(ys_hbm, b1_ref, 0, 0, sem).wait()
        return c

    lax.fori_loop(0, tm, drain, 0)
    gate = gate_ref[...]
    out = x_ref[...] + gate[:, 0:1] * b0_ref[...] + gate[:, 1:2] * b1_ref[...]
    if final_norm:
        out = _rms(out, gf_ref[...])
    o_ref[...] = out


def _moe_combine(x1, gate, ys, pos, gfin, tm, final_norm):
    t, d = x1.shape
    pos3 = pos.reshape(t // tm, 1, 2 * tm)
    return pl.pallas_call(
        functools.partial(_moe_combine_kernel, tm=tm, final_norm=final_norm), grid=(t // tm,),
        in_specs=[pl.BlockSpec((1, 1, 2 * tm), lambda i: (i, 0, 0), memory_space=pltpu.SMEM),
                  pl.BlockSpec((tm, d), lambda i: (i, 0)),
                  pl.BlockSpec((tm, LANES), lambda i: (i, 0)),
                  pl.BlockSpec(memory_space=pl.ANY), _resident(gfin.shape)],
        out_specs=pl.BlockSpec((tm, d), lambda i: (i, 0)),
        out_shape=jax.ShapeDtypeStruct((t, d), F32),
        scratch_shapes=[pltpu.VMEM((tm, d), F32), pltpu.VMEM((tm, d), F32), pltpu.SemaphoreType.DMA(())],
        compiler_params=_cparams("arbitrary"), name="moe_combine",
    )(pos3, x1, gate, ys, gfin)


def _rmsnorm_kernel(x_ref, g_ref, o_ref):
    o_ref[...] = _rms(x_ref[...], g_ref[...])


def _rmsnorm(x, g, tm):
    t, d = x.shape
    return pl.pallas_call(
        _rmsnorm_kernel, grid=(t // tm,),
        in_specs=[pl.BlockSpec((tm, d), lambda i: (i, 0)), _resident(g.shape)],
        out_specs=pl.BlockSpec((tm, d), lambda i: (i, 0)),
        out_shape=jax.ShapeDtypeStruct((t, d), F32),
        compiler_params=_cparams("parallel"), name="final_norm",
    )(x, g)


def _rot_cols(w):
    half = QK_ROPE // 2
    return jnp.concatenate([-w[..., half:], w[..., :half]], axis=-1)


def _prep_mla(w_dq, q_norm, w_uq, w_dkv, kv_norm, w_ukv, w_o):
    ql = w_dq.shape[1]
    kvl = w_ukv.shape[0]
    wq = w_uq.reshape(ql, N_HEADS, QK_NOPE + QK_ROPE)
    wq_nope, wq_rope = wq[..., :QK_NOPE], wq[..., QK_NOPE:]
    zpad = jnp.zeros((ql, N_HEADS, HEAD_PAD - QK_NOPE - QK_ROPE), F32)
    w1 = jnp.concatenate([wq_nope, wq_rope, zpad], axis=-1).reshape(ql, N_HEADS * HEAD_PAD)
    w2 = jnp.concatenate([jnp.zeros_like(wq_nope), _rot_cols(wq_rope), zpad], axis=-1).reshape(ql, N_HEADS * HEAD_PAD)
    kr_w = w_dkv[:, kvl:]
    lpad = jnp.zeros((w_dkv.shape[0], LANES - QK_ROPE), F32)
    wdkv = jnp.concatenate([w_dkv[:, :kvl], kr_w, lpad, _rot_cols(kr_w), lpad], axis=1)
    wkv = w_ukv.reshape(kvl, N_HEADS, QK_NOPE + V_DIM)
    wk_top = jnp.concatenate([wkv[..., :QK_NOPE], jnp.zeros((kvl, N_HEADS, HEAD_PAD - QK_NOPE), F32)], axis=-1)
    eye = jnp.eye(LANES, QK_ROPE, dtype=F32)
    wk_bot = jnp.concatenate([jnp.zeros((LANES, QK_NOPE), F32), eye,
                              jnp.zeros((LANES, HEAD_PAD - QK_NOPE - QK_ROPE), F32)], axis=-1)
    wk_bot = jnp.broadcast_to(wk_bot[:, None, :], (LANES, N_HEADS, HEAD_PAD))
    wk = jnp.concatenate([wk_top, wk_bot], axis=0).reshape(kvl + LANES, N_HEADS * HEAD_PAD)
    wvt = wkv[..., QK_NOPE:].reshape(kvl, N_HEADS * V_DIM).T
    return dict(wdq=w_dq.astype(BF16), qn=q_norm.reshape(1, -1), w1t=w1.T.astype(BF16), w2t=w2.T.astype(BF16),
                wdkv=wdkv.astype(BF16), kvn=kv_norm.reshape(1, -1), wk=wk.astype(BF16), wvt=wvt.astype(BF16),
                wo=w_o.astype(BF16))


def _rope_tables(s):
    pos = jnp.arange(s, dtype=F32)
    inv = ROPE_THETA ** (-jnp.arange(0, QK_ROPE, 2, dtype=F32) / QK_ROPE)
    ang = pos[:, None] * inv[None, :]
    ang = jnp.concatenate([ang, ang], axis=-1)
    cos, sin = jnp.cos(ang), jnp.sin(ang)
    qs = ATTN_SCALE * LOG2E
    zq = jnp.zeros((s, HEAD_PAD - QK_NOPE - QK_ROPE), F32)
    ctq = jnp.concatenate([jnp.full((s, QK_NOPE), qs, F32), cos * qs, zq], axis=1).T
    stq = jnp.concatenate([jnp.zeros((s, QK_NOPE), F32), sin * qs, zq], axis=1).T
    zk = jnp.zeros((s, LANES - QK_ROPE), F32)
    ctk = jnp.concatenate([cos, zk], axis=1)
    stk = jnp.concatenate([sin, zk], axis=1)
    return ctq, stq, ctk, stk


def _prep_s5(a_re, a_im, log_step, b_re, b_im, c_re, c_im):
    _, g, p = a_re.shape
    n = g * p * GROUP_CH
    rep = lambda a: jnp.broadcast_to(a[..., None], (2, g, p, GROUP_CH)).reshape(2, n)
    ls = jnp.broadcast_to(log_step[:, :, None, None], (2, g, p, GROUP_CH)).reshape(2, n)
    outs = pl.pallas_call(
        _s5_param_kernel,
        out_shape=[jax.ShapeDtypeStruct((2, n), F32)] * 4, name="s5_params",
    )(rep(a_re), rep(a_im), ls, b_re.reshape(2, n), b_im.reshape(2, n))
    lr, li, bbr, bbi = [o.reshape(2, g, p, GROUP_CH) for o in outs]
    lr, li = lr[..., 0], li[..., 0]
    gpt = MXU_DIM // GROUP_CH
    ntile = g // gpt
    eye = jnp.eye(gpt, dtype=F32)

    def tiles_b(bb):
        t = bb.reshape(2, ntile, gpt, p, GROUP_CH)
        return jnp.einsum('djapc,ab->djacbp', t, eye).reshape(2, ntile, gpt * GROUP_CH, gpt * p)

    def tiles_c(cc):
        t = cc.reshape(2, ntile, gpt, GROUP_CH, p)
        return jnp.einsum('djacp,ab->djapbc', t, eye).reshape(2, ntile, gpt * p, gpt * GROUP_CH)

    bt = jnp.concatenate([tiles_b(bbr), tiles_b(bbi)], axis=-1).astype(BF16)
    ct = jnp.concatenate([tiles_c(c_re), tiles_c(-c_im)], axis=-2).astype(BF16)
    lrt = lr.reshape(2, ntile, 1, gpt * p)
    lit = li.reshape(2, ntile, 1, gpt * p)
    return bt, ct, lrt, lit


def _cpow(lr, li, n):
    k = int(math.log2(n))
    assert 2 ** k == n
    for _ in range(k):
        lr, li = lr * lr - li * li, 2.0 * lr * li
    return lr, li


def _mla_layer(x, p, j, tabs, tm, tq, tk):
    w = p['mla'][j]
    qt, k, vt = _mla_pre(x, p['norm_mix'][2 * j], w, tabs, tm)
    ot = _attention(qt, k, vt, tq, tk)
    f = p['ffn'][j]
    return _attn_out_ffn(x, ot, w['wo'], p['norm_ffn'][2 * j], f['wg'], f['wu'], f['wd'], tm)


def _s5_states(xp, g, s5, nseg, tt):
    bt, ct, lr, li = s5
    nrg = xp.shape[0]
    ntile, half = bt.shape[1], bt.shape[3] // 2
    zero = jnp.zeros((nrg, ntile, 2, SUBLANES, half), F32)
    if nseg == 1:
        return zero, zero
    seg_len = xp.shape[1] // SUBLANES
    inits = []
    for k, reverse in ((0, False), (1, True)):
        _, fin = _s5_scan(xp, g, bt[k], ct[k], lr[k], li[k], zero, tt, reverse, False)
        e = fin.reshape(nrg, ntile, 2, SUBLANES // nseg, nseg, half)
        plr, pli = _cpow(lr[k], li[k], seg_len)
        plr, pli = plr[None], pli[None]
        order = range(nseg - 1, -1, -1) if reverse else range(nseg)
        sr = jnp.zeros_like(e[:, :, 0, :, 0])
        si = jnp.zeros_like(sr)
        init_r = [None] * nseg
        init_i = [None] * nseg
        for q in order:
            init_r[q], init_i[q] = sr, si
            er, ei = e[:, :, 0, :, q], e[:, :, 1, :, q]
            sr, si = plr * sr - pli * si + er, plr * si + pli * sr + ei
        init = jnp.stack([jnp.stack(init_r, axis=3), jnp.stack(init_i, axis=3)], axis=2)
        inits.append(init.reshape(nrg, ntile, 2, SUBLANES, half))
    return inits[0], inits[1]


def _s5_moe_layer(xp, p, j, nseg, tt, tm, last):
    nrg, rows, d = xp.shape
    i = 2 * j + 1
    s5 = p['s5'][j]
    bt, ct, lr, li = s5
    gm = p['norm_mix'][i]
    init_f, init_b = _s5_states(xp, gm, s5, nseg, tt)
    yf, _ = _s5_scan(xp, gm, bt[0], ct[0], lr[0], li[0], init_f, tt, False, True)
    yb, _ = _s5_scan(xp, gm, bt[1], ct[1], lr[1], li[1], init_b, tt, True, True)
    t = nrg * rows
    m = p['moe'][j]
    x1, h2, idx, gate = _s5_glu_router(xp.reshape(t, d), yf.reshape(t, d), yb.reshape(t, d), gm,
                                       p['s5_d'][j], p['s5_wglu'][j], p['norm_ffn'][i], m['wr'], tm)
    rank, cnt = _moe_plan(idx, tm)
    counts = cnt[0, :N_EXPERTS].astype(jnp.int32)
    ntiles = (counts + tm - 1) // tm
    tile_end = jnp.cumsum(ntiles)
    offs = (tile_end - ntiles) * tm
    e2 = idx[:, :2]
    pos = offs[e2] + rank[:, :2]
    nt_max = (2 * t) // tm + N_EXPERTS
    tile_expert = jnp.minimum(jnp.searchsorted(tile_end, jnp.arange(nt_max, dtype=jnp.int32), side='right'),
                              N_EXPERTS - 1).astype(jnp.int32)
    n_used = tile_end[-1:].astype(jnp.int32)
    xs = _moe_dispatch(h2, pos, nt_max * tm, tm)
    ys = _moe_experts(xs, tile_expert, n_used, m['wg'], m['wu'], m['wd'], tm)
    out = _moe_combine(x1, gate, ys, pos, p['norm_final'], tm, last)
    return out.reshape(nrg, rows, d)


def _to_perm(x, nseg):
    b, s, d = x.shape
    nrg = b * nseg // SUBLANES
    xr = x.reshape(nrg, SUBLANES, s // nseg, d)
    return jnp.swapaxes(xr, 1, 2).reshape(nrg, (s // nseg) * SUBLANES, d)


def _from_perm(xp, b, s, nseg):
    nrg, rows, d = xp.shape
    xr = xp.reshape(nrg, rows // SUBLANES, SUBLANES, d)
    return jnp.swapaxes(xr, 1, 2).reshape(b, s, d)


def _trunk(x, p):
    b, s, d = x.shape
    tm = _tile(s, 512)
    tq = _tile(s, 1024, LANES)
    tk = _tile(s, min(512, s // 4), LANES)
    nseg = 1 if b % SUBLANES == 0 else SUBLANES // b
    assert (b * nseg) % SUBLANES == 0 and s % nseg == 0
    tt = _tile(s // nseg, 64)
    tabs = _rope_tables(s)
    depth = p['norm_mix'].shape[0]
    for i in range(depth):
        j = i // 2
        if i % 2 == 0:
            x = _mla_layer(x, p, j, tabs, tm, tq, tk)
        else:
            xp = _s5_moe_layer(_to_perm(x, nseg), p, j, nseg, tt, tm, last=(i == depth - 1))
            x = _from_perm(xp, b, s, nseg)
    if depth % 2 == 1:
        x = _rmsnorm(x.reshape(b * s, d), p['norm_final'], tm).reshape(b, s, d)
    return x


def kernel(x_prompt, x_sample, norm_mix, norm_ffn, norm_final, mla_w_dq, mla_q_norm, mla_w_uq, mla_w_dkv,
           mla_kv_norm, mla_w_ukv, mla_w_o, ssm_a_re, ssm_a_im, ssm_log_step, ssm_b_re, ssm_b_im, ssm_c_re,
           ssm_c_im, ssm_d, ssm_w_glu, ffn_w_gate, ffn_w_up, ffn_w_down, moe_w_router, moe_w_gate, moe_w_up,
           moe_w_down):
    d = x_prompt.shape[-1]
    na, ns = mla_w_dq.shape[0], ssm_a_re.shape[0]
    p = dict(
        norm_mix=norm_mix.reshape(-1, 1, d), norm_ffn=norm_ffn.reshape(-1, 1, d), norm_final=norm_final.reshape(1, d),
        mla=[_prep_mla(mla_w_dq[j], mla_q_norm[j], mla_w_uq[j], mla_w_dkv[j], mla_kv_norm[j], mla_w_ukv[j],
                       mla_w_o[j]) for j in range(na)],
        ffn=[dict(wg=ffn_w_gate[j].astype(BF16), wu=ffn_w_up[j].astype(BF16), wd=ffn_w_down[j].astype(BF16))
             for j in range(na)],
        s5=[_prep_s5(ssm_a_re[j], ssm_a_im[j], ssm_log_step[j], ssm_b_re[j], ssm_b_im[j], ssm_c_re[j], ssm_c_im[j])
            for j in range(ns)],
        s5_d=[ssm_d[j].reshape(1, d) for j in range(ns)],
        s5_wglu=[ssm_w_glu[j].astype(BF16) for j in range(ns)],
        moe=[dict(wr=jnp.pad(moe_w_router[j], ((0, 0), (0, LANES - N_EXPERTS))),
                  wg=moe_w_gate[j].astype(BF16), wu=moe_w_up[j].astype(BF16), wd=moe_w_down[j].astype(BF16))
             for j in range(ns)],
    )
    return (_trunk(x_prompt, p), _trunk(x_sample, p))
```

```python
import functools
import math

import jax
import jax.numpy as jnp
from jax import lax
from jax.experimental import pallas as pl
from jax.experimental.pallas import tpu as pltpu

BF16 = jnp.bfloat16
F32 = jnp.float32

N_HEADS = 16
QK_NOPE = 64
QK_ROPE = 32
V_DIM = 64
HEAD_PAD = 128
ROPE_THETA = 10000.0
ATTN_SCALE = (QK_NOPE + QK_ROPE) ** -0.5
LOG2E = 1.4426950408889634
GROUP_CH = 16
STATE_DIM = 64
N_EXPERTS = 8
NORM_EPS = 1e-6
LANES = 128
SUBLANES = 8
MXU_DIM = 256
VMEM_LIMIT = 56 * 1024 * 1024
NEG_BIG = -1e30


def _cparams(*sem):
    return pltpu.CompilerParams(dimension_semantics=sem, vmem_limit_bytes=VMEM_LIMIT)


def _resident(shape):
    zeros = (0,) * len(shape)
    return pl.BlockSpec(shape, lambda *_: zeros, pipeline_mode=pl.Buffered(1))


def _tile(n, cap, mult=SUBLANES):
    if n <= cap:
        return n
    t = (cap // mult) * mult
    while t > mult and n % t:
        t -= mult
    assert n % t == 0, (n, cap, mult)
    return t


def _rms(x, g):
    ms = jnp.mean(x * x, axis=-1, keepdims=True)
    return x * lax.rsqrt(ms + NORM_EPS) * g


def _dot(a, b):
    return jnp.dot(a, b, preferred_element_type=F32)


def _dot_nt(a, b):
    return lax.dot_general(a, b, (((1,), (1,)), ((), ())), preferred_element_type=F32)


def _dot_tn(a, b):
    return lax.dot_general(a, b, (((0,), (0,)), ((), ())), preferred_element_type=F32)


def _swiglu(h, wg_ref, wu_ref, wd_ref, act_ref, fc):
    f = act_ref.shape[1]
    for c0 in range(0, f, fc):
        g = _dot(h, wg_ref[:, c0:c0 + fc])
        u = _dot(h, wu_ref[:, c0:c0 + fc])
        act_ref[:, c0:c0 + fc] = (g * jax.nn.sigmoid(g) * u).astype(BF16)
    return _dot(act_ref[...], wd_ref[...])


def _mla_pre_kernel(x_ref, g_ref, wdq_ref, qn_ref, w1_ref, w2_ref, wdkv_ref, kvn_ref, wk_ref, wv_ref,
                    ctq_ref, stq_ref, ctk_ref, stk_ref, qt_ref, k_ref, vt_ref):
    h = _rms(x_ref[0], g_ref[...]).astype(BF16)
    cq = _rms(_dot(h, wdq_ref[...]), qn_ref[...]).astype(BF16)
    a = _dot_nt(w1_ref[...], cq)
    b = _dot_nt(w2_ref[...], cq)
    ct = ctq_ref[...]
    st = stq_ref[...]
    for hh in range(N_HEADS):
        sl = slice(hh * HEAD_PAD, (hh + 1) * HEAD_PAD)
        qt_ref[0, sl, :] = (a[sl] * ct + b[sl] * st).astype(BF16)
    ck = _dot(h, wdkv_ref[...])
    kvl = kvn_ref.shape[1]
    ckv = _rms(ck[:, :kvl], kvn_ref[...]).astype(BF16)
    kr = (ck[:, kvl:kvl + LANES] * ctk_ref[...] + ck[:, kvl + LANES:] * stk_ref[...]).astype(BF16)
    kaug = jnp.concatenate([ckv, kr], axis=1)
    k_ref[0] = _dot(kaug, wk_ref[...]).astype(BF16)
    vt_ref[0] = _dot_nt(wv_ref[...], ckv).astype(BF16)


def _mla_pre(x, g, w, tabs, tm):
    b, s, d = x.shape
    hq = N_HEADS * HEAD_PAD
    hv = N_HEADS * V_DIM
    ctq, stq, ctk, stk = tabs
    grid = (b, s // tm)
    in_specs = [
        pl.BlockSpec((1, tm, d), lambda bi, i: (bi, i, 0)),
        _resident(g.shape), _resident(w['wdq'].shape), _resident(w['qn'].shape),
        _resident(w['w1t'].shape), _resident(w['w2t'].shape), _resident(w['wdkv'].shape),
        _resident(w['kvn'].shape), _resident(w['wk'].shape), _resident(w['wvt'].shape),
        pl.BlockSpec((HEAD_PAD, tm), lambda bi, i: (0, i)),
        pl.BlockSpec((HEAD_PAD, tm), lambda bi, i: (0, i)),
        pl.BlockSpec((tm, LANES), lambda bi, i: (i, 0)),
        pl.BlockSpec((tm, LANES), lambda bi, i: (i, 0)),
    ]
    out_specs = [
        pl.BlockSpec((1, hq, tm), lambda bi, i: (bi, 0, i)),
        pl.BlockSpec((1, tm, hq), lambda bi, i: (bi, i, 0)),
        pl.BlockSpec((1, hv, tm), lambda bi, i: (bi, 0, i)),
    ]
    out_shape = [jax.ShapeDtypeStruct((b, hq, s), BF16), jax.ShapeDtypeStruct((b, s, hq), BF16),
                 jax.ShapeDtypeStruct((b, hv, s), BF16)]
    return pl.pallas_call(
        _mla_pre_kernel, grid=grid, in_specs=in_specs, out_specs=out_specs, out_shape=out_shape,
        compiler_params=_cparams("parallel", "parallel"), name="mla_pre",
    )(x, g, w['wdq'], w['qn'], w['w1t'], w['w2t'], w['wdkv'], w['kvn'], w['wk'], w['wvt'], ctq, stq, ctk, stk)


SUM_ROWS = 16


def _attn_kernel(qt_ref, k_ref, vt_ref, o_ref, sa_ref, sb_ref, acc_ref, m_ref, *, tk):
    qt = qt_ref[0]
    nk = k_ref.shape[1] // tk
    ones = jnp.ones((SUM_ROWS, tk), BF16)

    def scores(j, dst_ref):
        k0 = pl.multiple_of(j * tk, tk)
        dst_ref[...] = _dot(k_ref[0, pl.ds(k0, tk), :], qt)

    def update(j, src_ref):
        k0 = pl.multiple_of(j * tk, tk)
        s = src_ref[...]
        m_old = m_ref[...]
        m_new = jnp.maximum(m_old, jnp.max(s, axis=0, keepdims=True))
        p = jnp.exp2(s - m_new).astype(BF16)
        v = jnp.concatenate([vt_ref[0, :, pl.ds(k0, tk)], ones], axis=0)
        acc_ref[...] = jnp.exp2(m_old - m_new) * acc_ref[...] + _dot(v, p)
        m_ref[...] = m_new

    m_ref[...] = jnp.full(m_ref.shape, NEG_BIG, F32)
    acc_ref[...] = jnp.zeros(acc_ref.shape, F32)
    scores(0, sa_ref)

    def pair(i, c):
        j = 2 * i
        scores(j + 1, sb_ref)
        update(j, sa_ref)
        scores(j + 2, sa_ref)
        update(j + 1, sb_ref)
        return c

    lax.fori_loop(0, nk // 2 - 1, pair, 0, unroll=2)
    scores(nk - 1, sb_ref)
    update(nk - 2, sa_ref)
    update(nk - 1, sb_ref)
    acc = acc_ref[...]
    o_ref[0] = (acc[:V_DIM] / acc[V_DIM:V_DIM + 1]).astype(BF16)


def _attention(qt, k, vt, tq, tk):
    b, hq, s = qt.shape
    assert (s // tk) % 2 == 0
    grid = (b, N_HEADS, s // tq)
    return pl.pallas_call(
        functools.partial(_attn_kernel, tk=tk), grid=grid,
        in_specs=[pl.BlockSpec((1, HEAD_PAD, tq), lambda bi, h, i: (bi, h, i)),
                  pl.BlockSpec((1, s, HEAD_PAD), lambda bi, h, i: (bi, 0, h)),
                  pl.BlockSpec((1, V_DIM, s), lambda bi, h, i: (bi, h, 0))],
        out_specs=pl.BlockSpec((1, V_DIM, tq), lambda bi, h, i: (bi, h, i)),
        out_shape=jax.ShapeDtypeStruct((b, N_HEADS * V_DIM, s), BF16),
        scratch_shapes=[pltpu.VMEM((tk, tq), F32), pltpu.VMEM((tk, tq), F32),
                        pltpu.VMEM((V_DIM + SUM_ROWS, tq), F32), pltpu.VMEM((1, tq), F32)],
        compiler_params=_cparams("parallel", "parallel", "parallel"), name="mla_attention",
    )(qt, k, vt)


def _attn_out_ffn_kernel(x_ref, ot_ref, wo_ref, g_ref, wg_ref, wu_ref, wd_ref, y_ref, act_ref, *, fc):
    x1 = x_ref[0] + _dot_tn(ot_ref[0], wo_ref[...])
    h = _rms(x1, g_ref[...]).astype(BF16)
    y_ref[0] = x1 + _swiglu(h, wg_ref, wu_ref, wd_ref, act_ref, fc)


def _attn_out_ffn(x, ot, wo, g, wg, wu, wd, tm):
    b, s, d = x.shape
    f = wg.shape[1]
    fc = _tile(f, 1536, LANES)
    hv = ot.shape[1]
    return pl.pallas_call(
        functools.partial(_attn_out_ffn_kernel, fc=fc), grid=(b, s // tm),
        in_specs=[pl.BlockSpec((1, tm, d), lambda bi, i: (bi, i, 0)),
                  pl.BlockSpec((1, hv, tm), lambda bi, i: (bi, 0, i)),
                  _resident(wo.shape), _resident(g.shape), _resident(wg.shape), _resident(wu.shape),
                  _resident(wd.shape)],
        out_specs=pl.BlockSpec((1, tm, d), lambda bi, i: (bi, i, 0)),
        out_shape=jax.ShapeDtypeStruct((b, s, d), F32),
        scratch_shapes=[pltpu.VMEM((tm, f), BF16)],
        compiler_params=_cparams("parallel", "parallel"), name="attn_out_ffn",
    )(x, ot, wo, g, wg, wu, wd)


def _s5_param_kernel(are_ref, aim_ref, ls_ref, bre_ref, bim_ref, lr_ref, li_ref, bbr_ref, bbi_ref):
    ar = are_ref[...]
    ai = aim_ref[...]
    delta = jnp.exp(ls_ref[...])
    mag = jnp.exp(ar * delta)
    lr = mag * jnp.cos(ai * delta)
    li = mag * jnp.sin(ai * delta)
    nr = lr - 1.0
    den = ar * ar + ai * ai
    cr = (nr * ar + li * ai) / den
    ci = (li * ar - nr * ai) / den
    br = bre_ref[...]
    bi = bim_ref[...]
    lr_ref[...] = lr
    li_ref[...] = li
    bbr_ref[...] = cr * br - ci * bi
    bbi_ref[...] = cr * bi + ci * br


def _s5_scan_kernel(x_ref, g_ref, bt_ref, ct_ref, lr_ref, li_ref, init_ref, *rest, tt, reverse, emit_y):
    if emit_y:
        y_ref, fin_ref, bu0_ref, bu1_ref, st_ref = rest
    else:
        fin_ref, bu0_ref, bu1_ref, st_ref = rest
    c = pl.program_id(1)
    ntile = bt_ref.shape[0]
    half = bt_ref.shape[2] // 2

    @pl.when(c == 0)
    def _():
        st_ref[...] = init_ref[0]

    hb = _rms(x_ref[0], g_ref[...]).astype(BF16)
    for j in range(ntile):
        bu_ref = (bu0_ref, bu1_ref)[j % 2]
        bu_ref[...] = _dot(hb[:, j * MXU_DIM:(j + 1) * MXU_DIM], bt_ref[j])
        lr = jnp.broadcast_to(lr_ref[j], (SUBLANES, half))
        li = jnp.broadcast_to(li_ref[j], (SUBLANES, half))
        xr, xi = st_ref[j, 0], st_ref[j, 1]
        for t in range(tt):
            r0 = ((tt - 1 - t) if reverse else t) * SUBLANES
            nr = lr * xr - li * xi + bu_ref[r0:r0 + SUBLANES, 0:half]
            ni = lr * xi + li * xr + bu_ref[r0:r0 + SUBLANES, half:2 * half]
            bu_ref[r0:r0 + SUBLANES, 0:half] = nr
            bu_ref[r0:r0 + SUBLANES, half:2 * half] = ni
            xr, xi = nr, ni
        st_ref[j, 0] = xr
        st_ref[j, 1] = xi
        if emit_y:
            y_ref[0, :, j * MXU_DIM:(j + 1) * MXU_DIM] = _dot(bu_ref[...].astype(BF16), ct_ref[j])

    @pl.when(c == pl.num_programs(1) - 1)
    def _():
        fin_ref[0] = st_ref[...]


def _s5_scan(xp, g, bt, ct, lr, li, init, tt, reverse, emit_y):
    nrg, rows, d = xp.shape
    s = rows // SUBLANES
    nc = s // tt
    ntile, _, sw = bt.shape
    half = sw // 2
    blk = tt * SUBLANES
    if reverse:
        xmap = lambda gi, c: (gi, nc - 1 - c, 0)
    else:
        xmap = lambda gi, c: (gi, c, 0)
    st_shape = (ntile, 2, SUBLANES, half)
    fin_spec = pl.BlockSpec((1,) + st_shape, lambda gi, c: (gi, 0, 0, 0, 0))
    fin_shape = jax.ShapeDtypeStruct((nrg,) + st_shape, F32)
    if emit_y:
        out_specs = [pl.BlockSpec((1, blk, d), xmap), fin_spec]
        out_shape = [jax.ShapeDtypeStruct(xp.shape, F32), fin_shape]
    else:
        out_specs = [fin_spec]
        out_shape = [fin_shape]
    outs = pl.pallas_call(
        functools.partial(_s5_scan_kernel, tt=tt, reverse=reverse, emit_y=emit_y), grid=(nrg, nc),
        in_specs=[pl.BlockSpec((1, blk, d), xmap), _resident(g.shape), _resident(bt.shape),
                  _resident(ct.shape), _resident(lr.shape), _resident(li.shape),
                  pl.BlockSpec((1,) + st_shape, lambda gi, c: (gi, 0, 0, 0, 0))],
        out_specs=out_specs, out_shape=out_shape,
        scratch_shapes=[pltpu.VMEM((blk, sw), F32), pltpu.VMEM((blk, sw), F32), pltpu.VMEM(st_shape, F32)],
        compiler_params=_cparams("parallel", "arbitrary"),
        name="s5_scan_" + ("bwd" if reverse else "fwd") + ("" if emit_y else "_state"),
    )(xp, g, bt, ct, lr, li, init)
    if emit_y:
        return outs[0], outs[1]
    return None, outs[0]


def _topk2(logits):
    lane = lax.broadcasted_iota(jnp.int32, logits.shape, 1)
    lg = jnp.where(lane < N_EXPERTS, logits, -jnp.inf)
    m1 = jnp.max(lg, axis=-1, keepdims=True)
    i1 = jnp.min(jnp.where(lg == m1, lane, LANES), axis=-1, keepdims=True)
    lg2 = jnp.where(lane == i1, -jnp.inf, lg)
    m2 = jnp.max(lg2, axis=-1, keepdims=True)
    i2 = jnp.min(jnp.where(lg2 == m2, lane, LANES), axis=-1, keepdims=True)
    e = jnp.exp(m2 - m1)
    g1 = 1.0 / (1.0 + e)
    g2 = e / (1.0 + e)
    idx = jnp.where(lane == 0, i1, jnp.where(lane == 1, i2, 0))
    gate = jnp.where(lane == 0, g1, jnp.where(lane == 1, g2, 0.0))
    return idx, gate


def _s5_glu_router_kernel(x_ref, yf_ref, yb_ref, gm_ref, d_ref, wglu_ref, gf_ref, wr_ref,
                          x1_ref, h2_ref, idx_ref, gate_ref):
    tm, d = x_ref.shape
    sub = tm // 2 if tm % (2 * SUBLANES) == 0 else tm
    for r0 in range(0, tm, sub):
        rows = slice(r0, r0 + sub)
        x = x_ref[rows, :]
        h = _rms(x, gm_ref[...])
        y = yf_ref[rows, :] + yb_ref[rows, :] + d_ref[...] * h
        g = jax.nn.gelu(y, approximate=True).astype(BF16)
        z = _dot(g, wglu_ref[...])
        x1 = x + z[:, :d] * jax.nn.sigmoid(z[:, d:])
        x1_ref[rows, :] = x1
        h2 = _rms(x1, gf_ref[...])
        h2_ref[rows, :] = h2
        hi = h2.astype(BF16)
        lo = (h2 - hi.astype(F32)).astype(BF16)
        logits = _dot(hi, wr_ref[0]) + (_dot(lo, wr_ref[0]) + _dot(hi, wr_ref[1]))
        idx, gate = _topk2(logits)
        idx_ref[rows, :] = idx
        gate_ref[rows, :] = gate


def _s5_glu_router(x, yf, yb, gm, dskip, wglu, gf, wr, tm):
    t, d = x.shape
    row = pl.BlockSpec((tm, d), lambda i: (i, 0))
    nar = pl.BlockSpec((tm, LANES), lambda i: (i, 0))
    return pl.pallas_call(
        _s5_glu_router_kernel, grid=(t // tm,),
        in_specs=[row, row, row, _resident(gm.shape), _resident(dskip.shape), _resident(wglu.shape),
                  _resident(gf.shape), _resident(wr.shape)],
        out_specs=[row, row, nar, nar],
        out_shape=[jax.ShapeDtypeStruct((t, d), F32), jax.ShapeDtypeStruct((t, d), F32),
                   jax.ShapeDtypeStruct((t, LANES), jnp.int32), jax.ShapeDtypeStruct((t, LANES), F32)],
        compiler_params=_cparams("parallel"), name="s5_glu_router",
    )(x, yf, yb, gm, dskip, wglu, gf, wr)


def _moe_plan_kernel(idx_ref, ltri_ref, rank_ref, cnt_ref, carry_ref):
    @pl.when(pl.program_id(0) == 0)
    def _():
        carry_ref[...] = jnp.zeros_like(carry_ref)

    idx = idx_ref[...]
    lane = lax.broadcasted_iota(jnp.int32, idx.shape, 1)
    oh1 = lane == idx[:, 0:1]
    oh2 = lane == idx[:, 1:2]
    oh = jnp.where(oh1, 1.0, jnp.where(oh2, 1.0, 0.0)).astype(BF16)
    cs = _dot(ltri_ref[...], oh)
    tot = carry_ref[...] + cs - 1.0
    r1 = jnp.sum(jnp.where(oh1, tot, 0.0), axis=-1, keepdims=True)
    r2 = jnp.sum(jnp.where(oh2, tot, 0.0), axis=-1, keepdims=True)
    rank_ref[...] = jnp.where(lane == 0, r1, jnp.where(lane == 1, r2, 0.0)).astype(jnp.int32)
    new = carry_ref[...] + cs[cs.shape[0] - 1:, :]
    carry_ref[...] = new
    cnt_ref[...] = new


def _moe_plan(idx, tm):
    t = idx.shape[0]
    ltri = jnp.tril(jnp.ones((tm, tm), F32)).astype(BF16)
    nar = pl.BlockSpec((tm, LANES), lambda i: (i, 0))
    return pl.pallas_call(
        _moe_plan_kernel, grid=(t // tm,),
        in_specs=[nar, _resident(ltri.shape)],
        out_specs=[nar, pl.BlockSpec((1, LANES), lambda i: (0, 0))],
        out_shape=[jax.ShapeDtypeStruct((t, LANES), jnp.int32), jax.ShapeDtypeStruct((1, LANES), F32)],
        scratch_shapes=[pltpu.VMEM((1, LANES), F32)],
        compiler_params=_cparams("arbitrary"), name="moe_plan",
    )(idx, ltri)


ROW_UNROLL = 8


def _row_copy(src, dst, i, j, sem):
    return pltpu.make_async_copy(src.at[pl.ds(i, 1)], dst.at[pl.ds(j, 1)], sem)


def _moe_dispatch_kernel(pos_ref, x_ref, xs_in, xs_hbm, sem, *, tm):
    del xs_in

    def issue(g, c):
        for u in range(ROW_UNROLL):
            r = g * ROW_UNROLL + u
            _row_copy(x_ref, xs_hbm, r, pos_ref[0, 0, 2 * r], sem).start(priority=0)
            _row_copy(x_ref, xs_hbm, r, pos_ref[0, 0, 2 * r + 1], sem).start(priority=1)
        return c

    lax.fori_loop(0, tm // ROW_UNROLL, issue, 0)

    def drain(g, c):
        for _ in range(2 * ROW_UNROLL):
            _row_copy(x_ref, xs_hbm, 0, 0, sem).wait()
        return c

    lax.fori_loop(0, tm // ROW_UNROLL, drain, 0)


def _moe_dispatch(h2, pos, rows, tm):
    t, d = h2.shape
    pos3 = pos.reshape(t // tm, 1, 2 * tm)
    xs0 = jnp.zeros((rows, d), F32)
    return pl.pallas_call(
        functools.partial(_moe_dispatch_kernel, tm=tm), grid=(t // tm,),
        in_specs=[pl.BlockSpec((1, 1, 2 * tm), lambda i: (i, 0, 0), memory_space=pltpu.SMEM),
                  pl.BlockSpec((tm, d), lambda i: (i, 0)), pl.BlockSpec(memory_space=pl.ANY)],
        out_specs=pl.BlockSpec(memory_space=pl.ANY),
        out_shape=jax.ShapeDtypeStruct((rows, d), F32),
        scratch_shapes=[pltpu.SemaphoreType.DMA(())],
        input_output_aliases={2: 0},
        compiler_params=_cparams("arbitrary"), name="moe_dispatch",
    )(pos3, h2, xs0)


def _moe_expert_kernel(te_ref, nu_ref, xs_ref, wg_ref, wu_ref, wd_ref, y_ref, act_ref, *, fc):
    i = pl.program_id(0)

    @pl.when(i < nu_ref[0])
    def _():
        y_ref[...] = _swiglu(xs_ref[...].astype(BF16), wg_ref.at[0], wu_ref.at[0], wd_ref.at[0], act_ref, fc)

    @pl.when(i >= nu_ref[0])
    def _():
        y_ref[...] = jnp.zeros_like(y_ref)


def _moe_experts(xs, tile_expert, n_used, wg, wu, wd, tm):
    rows, d = xs.shape
    f = wg.shape[2]
    fc = _tile(f, 1792, LANES)
    wspec = lambda shp: pl.BlockSpec((1,) + shp, lambda i, te, nu: (te[i], 0, 0), pipeline_mode=pl.Buffered(1))
    gs = pltpu.PrefetchScalarGridSpec(
        num_scalar_prefetch=2, grid=(rows // tm,),
        in_specs=[pl.BlockSpec((tm, d), lambda i, te, nu: (i, 0)),
                  wspec((d, f)), wspec((d, f)), wspec((f, d))],
        out_specs=pl.BlockSpec((tm, d), lambda i, te, nu: (i, 0)),
        scratch_shapes=[pltpu.VMEM((tm, f), BF16)])
    return pl.pallas_call(
        functools.partial(_moe_expert_kernel, fc=fc), grid_spec=gs,
        out_shape=jax.ShapeDtypeStruct((rows, d), F32),
        compiler_params=_cparams("arbitrary"), name="moe_experts",
    )(tile_expert, n_used, xs, wg, wu, wd)


def _moe_combine_kernel(pos_ref, x_ref, gate_ref, ys_hbm, gf_ref, o_ref, b0_ref, b1_ref, sem, *, tm, final_norm):
    def issue(g, c):
        for u in range(ROW_UNROLL):
            r = g * ROW_UNROLL + u
            _row_copy(ys_hbm, b0_ref, pos_ref[0, 0, 2 * r], r, sem).start(priority=0)
            _row_copy(ys_hbm, b1_ref, pos_ref[0, 0, 2 * r + 1], r, sem).start(priority=1)
        return c

    lax.fori_loop(0, tm // ROW_UNROLL, issue, 0)

    def drain(g, c):
        for _ in range(ROW_UNROLL):
            _row_copy(ys_hbm, b0_ref, 0, 0, sem).wait()
            _row_copy(ys_hbm, b1_ref, 0, 0, sem).wait()
        return c

    lax.fori_loop(0, tm // ROW_UNROLL, drain, 0)
    gate = gate_ref[...]
    out = x_ref[...] + gate[:, 0:1] * b0_ref[...] + gate[:, 1:2] * b1_ref[...]
    if final_norm:
        out = _rms(out, gf_ref[...])
    o_ref[...] = out


def _moe_combine(x1, gate, ys, pos, gfin, tm, final_norm):
    t, d = x1.shape
    pos3 = pos.reshape(t // tm, 1, 2 * tm)
    return pl.pallas_call(
        functools.partial(_moe_combine_kernel, tm=tm, final_norm=final_norm), grid=(t // tm,),
        in_specs=[pl.BlockSpec((1, 1, 2 * tm), lambda i: (i, 0, 0), memory_space=pltpu.SMEM),
                  pl.BlockSpec((tm, d), lambda i: (i, 0)),
                  pl.BlockSpec((tm, LANES), lambda i: (i, 0)),
                  pl.BlockSpec(memory_space=pl.ANY), _resident(gfin.shape)],
        out_specs=pl.BlockSpec((tm, d), lambda i: (i, 0)),
        out_shape=jax.ShapeDtypeStruct((t, d), F32),
        scratch_shapes=[pltpu.VMEM((tm, d), F32), pltpu.VMEM((tm, d), F32), pltpu.SemaphoreType.DMA(())],
        compiler_params=_cparams("arbitrary"), name="moe_combine",
    )(pos3, x1, gate, ys, gfin)


def _rmsnorm_kernel(x_ref, g_ref, o_ref):
    o_ref[...] = _rms(x_ref[...], g_ref[...])


def _rmsnorm(x, g, tm):
    t, d = x.shape
    return pl.pallas_call(
        _rmsnorm_kernel, grid=(t // tm,),
        in_specs=[pl.BlockSpec((tm, d), lambda i: (i, 0)), _resident(g.shape)],
        out_specs=pl.BlockSpec((tm, d), lambda i: (i, 0)),
        out_shape=jax.ShapeDtypeStruct((t, d), F32),
        compiler_params=_cparams("parallel"), name="final_norm",
    )(x, g)


def _rot_cols(w):
    half = QK_ROPE // 2
    return jnp.concatenate([-w[..., half:], w[..., :half]], axis=-1)


def _prep_mla(w_dq, q_norm, w_uq, w_dkv, kv_norm, w_ukv, w_o):
    ql = w_dq.shape[1]
    kvl = w_ukv.shape[0]
    wq = w_uq.reshape(ql, N_HEADS, QK_NOPE + QK_ROPE)
    wq_nope, wq_rope = wq[..., :QK_NOPE], wq[..., QK_NOPE:]
    zpad = jnp.zeros((ql, N_HEADS, HEAD_PAD - QK_NOPE - QK_ROPE), F32)
    w1 = jnp.concatenate([wq_nope, wq_rope, zpad], axis=-1).reshape(ql, N_HEADS * HEAD_PAD)
    w2 = jnp.concatenate([jnp.zeros_like(wq_nope), _rot_cols(wq_rope), zpad], axis=-1).reshape(ql, N_HEADS * HEAD_PAD)
    kr_w = w_dkv[:, kvl:]
    lpad = jnp.zeros((w_dkv.shape[0], LANES - QK_ROPE), F32)
    wdkv = jnp.concatenate([w_dkv[:, :kvl], kr_w, lpad, _rot_cols(kr_w), lpad], axis=1)
    wkv = w_ukv.reshape(kvl, N_HEADS, QK_NOPE + V_DIM)
    wk_top = jnp.concatenate([wkv[..., :QK_NOPE], jnp.zeros((kvl, N_HEADS, HEAD_PAD - QK_NOPE), F32)], axis=-1)
    eye = jnp.eye(LANES, QK_ROPE, dtype=F32)
    wk_bot = jnp.concatenate([jnp.zeros((LANES, QK_NOPE), F32), eye,
                              jnp.zeros((LANES, HEAD_PAD - QK_NOPE - QK_ROPE), F32)], axis=-1)
    wk_bot = jnp.broadcast_to(wk_bot[:, None, :], (LANES, N_HEADS, HEAD_PAD))
    wk = jnp.concatenate([wk_top, wk_bot], axis=0).reshape(kvl + LANES, N_HEADS * HEAD_PAD)
    wvt = wkv[..., QK_NOPE:].reshape(kvl, N_HEADS * V_DIM).T
    return dict(wdq=w_dq.astype(BF16), qn=q_norm.reshape(1, -1), w1t=w1.T.astype(BF16), w2t=w2.T.astype(BF16),
                wdkv=wdkv.astype(BF16), kvn=kv_norm.reshape(1, -1), wk=wk.astype(BF16), wvt=wvt.astype(BF16),
                wo=w_o.astype(BF16))


def _rope_tables(s):
    pos = jnp.arange(s, dtype=F32)
    inv = ROPE_THETA ** (-jnp.arange(0, QK_ROPE, 2, dtype=F32) / QK_ROPE)
    ang = pos[:, None] * inv[None, :]
    ang = jnp.concatenate([ang, ang], axis=-1)
    cos, sin = jnp.cos(ang), jnp.sin(ang)
    qs = ATTN_SCALE * LOG2E
    zq = jnp.zeros((s, HEAD_PAD - QK_NOPE - QK_ROPE), F32)
    ctq = jnp.concatenate([jnp.full((s, QK_NOPE), qs, F32), cos * qs, zq], axis=1).T
    stq = jnp.concatenate([jnp.zeros((s, QK_NOPE), F32), sin * qs, zq], axis=1).T
    zk = jnp.zeros((s, LANES - QK_ROPE), F32)
    ctk = jnp.concatenate([cos, zk], axis=1)
    stk = jnp.concatenate([sin, zk], axis=1)
    return ctq, stq, ctk, stk


def _prep_s5(a_re, a_im, log_step, b_re, b_im, c_re, c_im):
    _, g, p = a_re.shape
    n = g * p * GROUP_CH
    rep = lambda a: jnp.broadcast_to(a[..., None], (2, g, p, GROUP_CH)).reshape(2, n)
    ls = jnp.broadcast_to(log_step[:, :, None, None], (2, g, p, GROUP_CH)).reshape(2, n)
    outs = pl.pallas_call(
        _s5_param_kernel,
        out_shape=[jax.ShapeDtypeStruct((2, n), F32)] * 4, name="s5_params",
    )(rep(a_re), rep(a_im), ls, b_re.reshape(2, n), b_im.reshape(2, n))
    lr, li, bbr, bbi = [o.reshape(2, g, p, GROUP_CH) for o in outs]
    lr, li = lr[..., 0], li[..., 0]
    gpt = MXU_DIM // GROUP_CH
    ntile = g // gpt
    eye = jnp.eye(gpt, dtype=F32)

    def tiles_b(bb):
        t = bb.reshape(2, ntile, gpt, p, GROUP_CH)
        return jnp.einsum('djapc,ab->djacbp', t, eye).reshape(2, ntile, gpt * GROUP_CH, gpt * p)

    def tiles_c(cc):
        t = cc.reshape(2, ntile, gpt, GROUP_CH, p)
        return jnp.einsum('djacp,ab->djapbc', t, eye).reshape(2, ntile, gpt * p, gpt * GROUP_CH)

    bt = jnp.concatenate([tiles_b(bbr), tiles_b(bbi)], axis=-1).astype(BF16)
    ct = jnp.concatenate([tiles_c(c_re), tiles_c(-c_im)], axis=-2).astype(BF16)
    lrt = lr.reshape(2, ntile, 1, gpt * p)
    lit = li.reshape(2, ntile, 1, gpt * p)
    return bt, ct, lrt, lit


def _split_bf16(w):
    hi = w.astype(BF16)
    lo = (w - hi.astype(F32)).astype(BF16)
    return jnp.stack([hi, lo])


def _cpow(lr, li, n):
    k = int(math.log2(n))
    assert 2 ** k == n
    for _ in range(k):
        lr, li = lr * lr - li * li, 2.0 * lr * li
    return lr, li


def _mla_layer(x, p, j, tabs, tm, tq, tk):
    w = p['mla'][j]
    qt, k, vt = _mla_pre(x, p['norm_mix'][2 * j], w, tabs, tm)
    ot = _attention(qt, k, vt, tq, tk)
    f = p['ffn'][j]
    return _attn_out_ffn(x, ot, w['wo'], p['norm_ffn'][2 * j], f['wg'], f['wu'], f['wd'], tm)


def _s5_states(xp, g, s5, nseg, tt):
    bt, ct, lr, li = s5
    nrg = xp.shape[0]
    ntile, half = bt.shape[1], bt.shape[3] // 2
    zero = jnp.zeros((nrg, ntile, 2, SUBLANES, half), F32)
    if nseg == 1:
        return zero, zero
    seg_len = xp.shape[1] // SUBLANES
    inits = []
    for k, reverse in ((0, False), (1, True)):
        _, fin = _s5_scan(xp, g, bt[k], ct[k], lr[k], li[k], zero, tt, reverse, False)
        e = fin.reshape(nrg, ntile, 2, SUBLANES // nseg, nseg, half)
        plr, pli = _cpow(lr[k], li[k], seg_len)
        plr, pli = plr[None], pli[None]
        order = range(nseg - 1, -1, -1) if reverse else range(nseg)
        sr = jnp.zeros_like(e[:, :, 0, :, 0])
        si = jnp.zeros_like(sr)
        init_r = [None] * nseg
        init_i = [None] * nseg
        for q in order:
            init_r[q], init_i[q] = sr, si
            er, ei = e[:, :, 0, :, q], e[:, :, 1, :, q]
            sr, si = plr * sr - pli * si + er, plr * si + pli * sr + ei
        init = jnp.stack([jnp.stack(init_r, axis=3), jnp.stack(init_i, axis=3)], axis=2)
        inits.append(init.reshape(nrg, ntile, 2, SUBLANES, half))
    return inits[0], inits[1]


def _s5_moe_layer(xp, p, j, nseg, tt, tm, last):
    nrg, rows, d = xp.shape
    i = 2 * j + 1
    s5 = p['s5'][j]
    bt, ct, lr, li = s5
    gm = p['norm_mix'][i]
    init_f, init_b = _s5_states(xp, gm, s5, nseg, tt)
    yf, _ = _s5_scan(xp, gm, bt[0], ct[0], lr[0], li[0], init_f, tt, False, True)
    yb, _ = _s5_scan(xp, gm, bt[1], ct[1], lr[1], li[1], init_b, tt, True, True)
    t = nrg * rows
    m = p['moe'][j]
    x1, h2, idx, gate = _s5_glu_router(xp.reshape(t, d), yf.reshape(t, d), yb.reshape(t, d), gm,
                                       p['s5_d'][j], p['s5_wglu'][j], p['norm_ffn'][i], m['wr'], tm)
    rank, cnt = _moe_plan(idx, tm)
    counts = cnt[0, :N_EXPERTS].astype(jnp.int32)
    ntiles = (counts + tm - 1) // tm
    tile_end = jnp.cumsum(ntiles)
    offs = (tile_end - ntiles) * tm
    e2 = idx[:, :2]
    pos = offs[e2] + rank[:, :2]
    nt_max = (2 * t) // tm + N_EXPERTS
    tile_expert = jnp.minimum(jnp.searchsorted(tile_end, jnp.arange(nt_max, dtype=jnp.int32), side='right'),
                              N_EXPERTS - 1).astype(jnp.int32)
    n_used = tile_end[-1:].astype(jnp.int32)
    xs = _moe_dispatch(h2, pos, nt_max * tm, tm)
    ys = _moe_experts(xs, tile_expert, n_used, m['wg'], m['wu'], m['wd'], tm)
    out = _moe_combine(x1, gate, ys, pos, p['norm_final'], tm, last)
    return out.reshape(nrg, rows, d)


def _to_perm(x, nseg):
    b, s, d = x.shape
    nrg = b * nseg // SUBLANES
    xr = x.reshape(nrg, SUBLANES, s // nseg, d)
    return jnp.swapaxes(xr, 1, 2).reshape(nrg, (s // nseg) * SUBLANES, d)


def _from_perm(xp, b, s, nseg):
    nrg, rows, d = xp.shape
    xr = xp.reshape(nrg, rows // SUBLANES, SUBLANES, d)
    return jnp.swapaxes(xr, 1, 2).reshape(b, s, d)


def _trunk(x, p):
    b, s, d = x.shape
    tm = _tile(s, 512)
    tq = _tile(s, 1024, LANES)
    tk = _tile(s, min(512, s // 4), LANES)
    nseg = 1 if b % SUBLANES == 0 else SUBLANES // b
    assert (b * nseg) % SUBLANES == 0 and s % nseg == 0
    tt = _tile(s // nseg, 64)
    tabs = _rope_tables(s)
    depth = p['norm_mix'].shape[0]
    for i in range(depth):
        j = i // 2
        if i % 2 == 0:
            x = _mla_layer(x, p, j, tabs, tm, tq, tk)
        else:
            xp = _s5_moe_layer(_to_perm(x, nseg), p, j, nseg, tt, tm, last=(i == depth - 1))
            x = _from_perm(xp, b, s, nseg)
    if depth % 2 == 1:
        x = _rmsnorm(x.reshape(b * s, d), p['norm_final'], tm).reshape(b, s, d)
    return x


def kernel(x_prompt, x_sample, norm_mix, norm_ffn, norm_final, mla_w_dq, mla_q_norm, mla_w_uq, mla_w_dkv,
           mla_kv_norm, mla_w_ukv, mla_w_o, ssm_a_re, ssm_a_im, ssm_log_step, ssm_b_re, ssm_b_im, ssm_c_re,
           ssm_c_im, ssm_d, ssm_w_glu, ffn_w_gate, ffn_w_up, ffn_w_down, moe_w_router, moe_w_gate, moe_w_up,
           moe_w_down):
    d = x_prompt.shape[-1]
    na, ns = mla_w_dq.shape[0], ssm_a_re.shape[0]
    p = dict(
        norm_mix=norm_mix.reshape(-1, 1, d), norm_ffn=norm_ffn.reshape(-1, 1, d), norm_final=norm_final.reshape(1, d),
        mla=[_prep_mla(mla_w_dq[j], mla_q_norm[j], mla_w_uq[j], mla_w_dkv[j], mla_kv_norm[j], mla_w_ukv[j],
                       mla_w_o[j]) for j in range(na)],
        ffn=[dict(wg=ffn_w_gate[j].astype(BF16), wu=ffn_w_up[j].astype(BF16), wd=ffn_w_down[j].astype(BF16))
             for j in range(na)],
        s5=[_prep_s5(ssm_a_re[j], ssm_a_im[j], ssm_log_step[j], ssm_b_re[j], ssm_b_im[j], ssm_c_re[j], ssm_c_im[j])
            for j in range(ns)],
        s5_d=[ssm_d[j].reshape(1, d) for j in range(ns)],
        s5_wglu=[ssm_w_glu[j].astype(BF16) for j in range(ns)],
        moe=[dict(wr=_split_bf16(jnp.pad(moe_w_router[j], ((0, 0), (0, LANES - N_EXPERTS)))),
                  wg=moe_w_gate[j].astype(BF16), wu=moe_w_up[j].astype(BF16), wd=moe_w_down[j].astype(BF16))
             for j in range(ns)],
    )
    return (_trunk(x_prompt, p), _trunk(x_sample, p))
```

```python
import functools
import math

import jax
import jax.numpy as jnp
from jax import lax
from jax.experimental import pallas as pl
from jax.experimental.pallas import tpu as pltpu

BF16 = jnp.bfloat16
F32 = jnp.float32

N_HEADS = 16
QK_NOPE = 64
QK_ROPE = 32
V_DIM = 64
HEAD_PAD = 128
ROPE_THETA = 10000.0
ATTN_SCALE = (QK_NOPE + QK_ROPE) ** -0.5
LOG2E = 1.4426950408889634
GROUP_CH = 16
STATE_DIM = 64
N_EXPERTS = 8
NORM_EPS = 1e-6
LANES = 128
SUBLANES = 8
MXU_DIM = 256
VMEM_LIMIT = 56 * 1024 * 1024
NEG_BIG = -1e30


def _cparams(*sem):
    return pltpu.CompilerParams(dimension_semantics=sem, vmem_limit_bytes=VMEM_LIMIT)


def _resident(shape):
    zeros = (0,) * len(shape)
    return pl.BlockSpec(shape, lambda *_: zeros, pipeline_mode=pl.Buffered(1))


def _tile(n, cap, mult=SUBLANES):
    if n <= cap:
        return n
    t = (cap // mult) * mult
    while t > mult and n % t:
        t -= mult
    assert n % t == 0, (n, cap, mult)
    return t


def _rms(x, g):
    ms = jnp.mean(x * x, axis=-1, keepdims=True)
    return x * lax.rsqrt(ms + NORM_EPS) * g


def _dot(a, b):
    return jnp.dot(a, b, preferred_element_type=F32)


def _dot_nt(a, b):
    return lax.dot_general(a, b, (((1,), (1,)), ((), ())), preferred_element_type=F32)


def _dot_tn(a, b):
    return lax.dot_general(a, b, (((0,), (0,)), ((), ())), preferred_element_type=F32)


def _swiglu(h, wg_ref, wu_ref, wd_ref, act_ref, fc):
    f = act_ref.shape[1]
    for c0 in range(0, f, fc):
        g = _dot(h, wg_ref[:, c0:c0 + fc])
        u = _dot(h, wu_ref[:, c0:c0 + fc])
        act_ref[:, c0:c0 + fc] = (g * jax.nn.sigmoid(g) * u).astype(BF16)
    return _dot(act_ref[...], wd_ref[...])


def _mla_pre_kernel(x_ref, g_ref, wdq_ref, qn_ref, w1_ref, w2_ref, wdkv_ref, kvn_ref, wk_ref, wv_ref,
                    ctq_ref, stq_ref, ctk_ref, stk_ref, qt_ref, k_ref, vt_ref):
    tm = x_ref.shape[1]
    sub = tm
    kvl = kvn_ref.shape[1]
    for r0 in range(0, tm, sub):
        rows = slice(r0, r0 + sub)
        h = _rms(x_ref[0, rows, :], g_ref[...]).astype(BF16)
        cq = _rms(_dot(h, wdq_ref[...]), qn_ref[...]).astype(BF16)
        a = _dot_nt(w1_ref[...], cq)
        b = _dot_nt(w2_ref[...], cq)
        ct = ctq_ref[:, rows]
        st = stq_ref[:, rows]
        for hh in range(N_HEADS):
            sl = slice(hh * HEAD_PAD, (hh + 1) * HEAD_PAD)
            qt_ref[0, sl, rows] = (a[sl] * ct + b[sl] * st).astype(BF16)
        ck = _dot(h, wdkv_ref[...])
        ckv = _rms(ck[:, :kvl], kvn_ref[...]).astype(BF16)
        kr = (ck[:, kvl:kvl + LANES] * ctk_ref[rows, :] + ck[:, kvl + LANES:] * stk_ref[rows, :]).astype(BF16)
        kaug = jnp.concatenate([ckv, kr], axis=1)
        k_ref[0, rows, :] = _dot(kaug, wk_ref[...]).astype(BF16)
        vt_ref[0, :, rows] = _dot_nt(wv_ref[...], ckv).astype(BF16)


def _mla_pre(x, g, w, tabs, tm):
    b, s, d = x.shape
    hq = N_HEADS * HEAD_PAD
    hv = N_HEADS * V_DIM
    ctq, stq, ctk, stk = tabs
    grid = (b, s // tm)
    in_specs = [
        pl.BlockSpec((1, tm, d), lambda bi, i: (bi, i, 0)),
        _resident(g.shape), _resident(w['wdq'].shape), _resident(w['qn'].shape),
        _resident(w['w1t'].shape), _resident(w['w2t'].shape), _resident(w['wdkv'].shape),
        _resident(w['kvn'].shape), _resident(w['wk'].shape), _resident(w['wvt'].shape),
        pl.BlockSpec((HEAD_PAD, tm), lambda bi, i: (0, i)),
        pl.BlockSpec((HEAD_PAD, tm), lambda bi, i: (0, i)),
        pl.BlockSpec((tm, LANES), lambda bi, i: (i, 0)),
        pl.BlockSpec((tm, LANES), lambda bi, i: (i, 0)),
    ]
    out_specs = [
        pl.BlockSpec((1, hq, tm), lambda bi, i: (bi, 0, i)),
        pl.BlockSpec((1, tm, hq), lambda bi, i: (bi, i, 0)),
        pl.BlockSpec((1, hv, tm), lambda bi, i: (bi, 0, i)),
    ]
    out_shape = [jax.ShapeDtypeStruct((b, hq, s), BF16), jax.ShapeDtypeStruct((b, s, hq), BF16),
                 jax.ShapeDtypeStruct((b, hv, s), BF16)]
    return pl.pallas_call(
        _mla_pre_kernel, grid=grid, in_specs=in_specs, out_specs=out_specs, out_shape=out_shape,
        compiler_params=_cparams("parallel", "parallel"), name="mla_pre",
    )(x, g, w['wdq'], w['qn'], w['w1t'], w['w2t'], w['wdkv'], w['kvn'], w['wk'], w['wvt'], ctq, stq, ctk, stk)


SUM_ROWS = 16


def _attn_kernel(qt_ref, k_ref, vt_ref, o_ref, sa_ref, sb_ref, acc_ref, m_ref, *, tk, unroll):
    qt = qt_ref[0]
    nk = k_ref.shape[1] // tk
    ones = jnp.ones((SUM_ROWS, tk), BF16)

    def scores(j, dst_ref):
        k0 = pl.multiple_of(j * tk, tk)
        dst_ref[...] = _dot(k_ref[0, pl.ds(k0, tk), :], qt)

    def update(j, src_ref):
        k0 = pl.multiple_of(j * tk, tk)
        s = src_ref[...]
        m_old = m_ref[...]
        m_new = jnp.maximum(m_old, jnp.max(s, axis=0, keepdims=True))
        p = jnp.exp2(s - m_new).astype(BF16)
        v = jnp.concatenate([vt_ref[0, :, pl.ds(k0, tk)], ones], axis=0)
        acc_ref[...] = jnp.exp2(m_old - m_new) * acc_ref[...] + _dot(v, p)
        m_ref[...] = m_new

    m_ref[...] = jnp.full(m_ref.shape, NEG_BIG, F32)
    acc_ref[...] = jnp.zeros(acc_ref.shape, F32)
    scores(0, sa_ref)

    def pair(i, c):
        j = 2 * i
        scores(j + 1, sb_ref)
        update(j, sa_ref)
        scores(j + 2, sa_ref)
        update(j + 1, sb_ref)
        return c

    lax.fori_loop(0, nk // 2 - 1, pair, 0, unroll=unroll)
    scores(nk - 1, sb_ref)
    update(nk - 2, sa_ref)
    update(nk - 1, sb_ref)
    acc = acc_ref[...]
    o_ref[0] = (acc[:V_DIM] / acc[V_DIM:V_DIM + 1]).astype(BF16)


def _attention(qt, k, vt, tq, tk, unroll):
    b, hq, s = qt.shape
    assert (s // tk) % 2 == 0
    grid = (b, N_HEADS, s // tq)
    return pl.pallas_call(
        functools.partial(_attn_kernel, tk=tk, unroll=unroll), grid=grid,
        in_specs=[pl.BlockSpec((1, HEAD_PAD, tq), lambda bi, h, i: (bi, h, i)),
                  pl.BlockSpec((1, s, HEAD_PAD), lambda bi, h, i: (bi, 0, h)),
                  pl.BlockSpec((1, V_DIM, s), lambda bi, h, i: (bi, h, 0))],
        out_specs=pl.BlockSpec((1, V_DIM, tq), lambda bi, h, i: (bi, h, i)),
        out_shape=jax.ShapeDtypeStruct((b, N_HEADS * V_DIM, s), BF16),
        scratch_shapes=[pltpu.VMEM((tk, tq), F32), pltpu.VMEM((tk, tq), F32),
                        pltpu.VMEM((V_DIM + SUM_ROWS, tq), F32), pltpu.VMEM((1, tq), F32)],
        compiler_params=_cparams("parallel", "parallel", "parallel"), name="mla_attention",
    )(qt, k, vt)


def _attn_out_ffn_kernel(x_ref, ot_ref, wo_ref, g_ref, wg_ref, wu_ref, wd_ref, y_ref, act_ref, *, fc):
    x1 = x_ref[0] + _dot_tn(ot_ref[0], wo_ref[...])
    h = _rms(x1, g_ref[...]).astype(BF16)
    y_ref[0] = x1 + _swiglu(h, wg_ref, wu_ref, wd_ref, act_ref, fc)


def _attn_out_ffn(x, ot, wo, g, wg, wu, wd, tm):
    b, s, d = x.shape
    f = wg.shape[1]
    fc = _tile(f, 1536, LANES)
    hv = ot.shape[1]
    return pl.pallas_call(
        functools.partial(_attn_out_ffn_kernel, fc=fc), grid=(b, s // tm),
        in_specs=[pl.BlockSpec((1, tm, d), lambda bi, i: (bi, i, 0)),
                  pl.BlockSpec((1, hv, tm), lambda bi, i: (bi, 0, i)),
                  _resident(wo.shape), _resident(g.shape), _resident(wg.shape), _resident(wu.shape),
                  _resident(wd.shape)],
        out_specs=pl.BlockSpec((1, tm, d), lambda bi, i: (bi, i, 0)),
        out_shape=jax.ShapeDtypeStruct((b, s, d), F32),
        scratch_shapes=[pltpu.VMEM((tm, f), BF16)],
        compiler_params=_cparams("parallel", "parallel"), name="attn_out_ffn",
    )(x, ot, wo, g, wg, wu, wd)


def _s5_param_kernel(are_ref, aim_ref, ls_ref, bre_ref, bim_ref, lr_ref, li_ref, bbr_ref, bbi_ref):
    ar = are_ref[...]
    ai = aim_ref[...]
    delta = jnp.exp(ls_ref[...])
    mag = jnp.exp(ar * delta)
    lr = mag * jnp.cos(ai * delta)
    li = mag * jnp.sin(ai * delta)
    nr = lr - 1.0
    den = ar * ar + ai * ai
    cr = (nr * ar + li * ai) / den
    ci = (li * ar - nr * ai) / den
    br = bre_ref[...]
    bi = bim_ref[...]
    lr_ref[...] = lr
    li_ref[...] = li
    bbr_ref[...] = cr * br - ci * bi
    bbi_ref[...] = cr * bi + ci * br


def _s5_scan_kernel(x_ref, g_ref, bt_ref, ct_ref, lr_ref, li_ref, init_ref, *rest, tt, reverse, emit_y):
    if emit_y:
        y_ref, fin_ref, bu0_ref, bu1_ref, st_ref = rest
    else:
        fin_ref, bu0_ref, bu1_ref, st_ref = rest
    c = pl.program_id(1)
    ntile = bt_ref.shape[0]
    half = bt_ref.shape[2] // 2

    @pl.when(c == 0)
    def _():
        st_ref[...] = init_ref[0]

    hb = _rms(x_ref[0], g_ref[...]).astype(BF16)
    for j in range(ntile):
        bu_ref = (bu0_ref, bu1_ref)[j % 2]
        bu_ref[...] = _dot(hb[:, j * MXU_DIM:(j + 1) * MXU_DIM], bt_ref[j])
        lr = jnp.broadcast_to(lr_ref[j], (SUBLANES, half))
        li = jnp.broadcast_to(li_ref[j], (SUBLANES, half))
        xr, xi = st_ref[j, 0], st_ref[j, 1]
        for t in range(tt):
            r0 = ((tt - 1 - t) if reverse else t) * SUBLANES
            nr = lr * xr - li * xi + bu_ref[r0:r0 + SUBLANES, 0:half]
            ni = lr * xi + li * xr + bu_ref[r0:r0 + SUBLANES, half:2 * half]
            bu_ref[r0:r0 + SUBLANES, 0:half] = nr
            bu_ref[r0:r0 + SUBLANES, half:2 * half] = ni
            xr, xi = nr, ni
        st_ref[j, 0] = xr
        st_ref[j, 1] = xi
        if emit_y:
            y_ref[0, :, j * MXU_DIM:(j + 1) * MXU_DIM] = _dot(bu_ref[...].astype(BF16), ct_ref[j])

    @pl.when(c == pl.num_programs(1) - 1)
    def _():
        fin_ref[0] = st_ref[...]


def _s5_scan(xp, g, bt, ct, lr, li, init, tt, reverse, emit_y):
    nrg, rows, d = xp.shape
    s = rows // SUBLANES
    nc = s // tt
    ntile, _, sw = bt.shape
    half = sw // 2
    blk = tt * SUBLANES
    if reverse:
        xmap = lambda gi, c: (gi, nc - 1 - c, 0)
    else:
        xmap = lambda gi, c: (gi, c, 0)
    st_shape = (ntile, 2, SUBLANES, half)
    fin_spec = pl.BlockSpec((1,) + st_shape, lambda gi, c: (gi, 0, 0, 0, 0))
    fin_shape = jax.ShapeDtypeStruct((nrg,) + st_shape, F32)
    if emit_y:
        out_specs = [pl.BlockSpec((1, blk, d), xmap), fin_spec]
        out_shape = [jax.ShapeDtypeStruct(xp.shape, F32), fin_shape]
    else:
        out_specs = [fin_spec]
        out_shape = [fin_shape]
    outs = pl.pallas_call(
        functools.partial(_s5_scan_kernel, tt=tt, reverse=reverse, emit_y=emit_y), grid=(nrg, nc),
        in_specs=[pl.BlockSpec((1, blk, d), xmap), _resident(g.shape), _resident(bt.shape),
                  _resident(ct.shape), _resident(lr.shape), _resident(li.shape),
                  pl.BlockSpec((1,) + st_shape, lambda gi, c: (gi, 0, 0, 0, 0))],
        out_specs=out_specs, out_shape=out_shape,
        scratch_shapes=[pltpu.VMEM((blk, sw), F32), pltpu.VMEM((blk, sw), F32), pltpu.VMEM(st_shape, F32)],
        compiler_params=_cparams("parallel", "arbitrary"),
        name="s5_scan_" + ("bwd" if reverse else "fwd") + ("" if emit_y else "_state"),
    )(xp, g, bt, ct, lr, li, init)
    if emit_y:
        return outs[0], outs[1]
    return None, outs[0]


def _topk2(logits):
    lane = lax.broadcasted_iota(jnp.int32, logits.shape, 1)
    lg = jnp.where(lane < N_EXPERTS, logits, -jnp.inf)
    m1 = jnp.max(lg, axis=-1, keepdims=True)
    i1 = jnp.min(jnp.where(lg == m1, lane, LANES), axis=-1, keepdims=True)
    lg2 = jnp.where(lane == i1, -jnp.inf, lg)
    m2 = jnp.max(lg2, axis=-1, keepdims=True)
    i2 = jnp.min(jnp.where(lg2 == m2, lane, LANES), axis=-1, keepdims=True)
    e = jnp.exp(m2 - m1)
    g1 = 1.0 / (1.0 + e)
    g2 = e / (1.0 + e)
    idx = jnp.where(lane == 0, i1, jnp.where(lane == 1, i2, 0))
    gate = jnp.where(lane == 0, g1, jnp.where(lane == 1, g2, 0.0))
    return idx, gate


def _s5_glu_router_kernel(x_ref, yf_ref, yb_ref, gm_ref, d_ref, wglu_ref, gf_ref, wr_ref,
                          x1_ref, h2_ref, idx_ref, gate_ref):
    tm, d = x_ref.shape
    sub = tm // 2 if tm % (2 * SUBLANES) == 0 else tm
    for r0 in range(0, tm, sub):
        rows = slice(r0, r0 + sub)
        x = x_ref[rows, :]
        h = _rms(x, gm_ref[...])
        y = yf_ref[rows, :] + yb_ref[rows, :] + d_ref[...] * h
        g = jax.nn.gelu(y, approximate=True).astype(BF16)
        z = _dot(g, wglu_ref[...])
        x1 = x + z[:, :d] * jax.nn.sigmoid(z[:, d:])
        x1_ref[rows, :] = x1
        h2 = _rms(x1, gf_ref[...])
        h2_ref[rows, :] = h2
        hi = h2.astype(BF16)
        lo = (h2 - hi.astype(F32)).astype(BF16)
        logits = _dot(hi, wr_ref[0]) + (_dot(lo, wr_ref[0]) + _dot(hi, wr_ref[1]))
        idx, gate = _topk2(logits)
        idx_ref[rows, :] = idx
        gate_ref[rows, :] = gate


def _s5_glu_router(x, yf, yb, gm, dskip, wglu, gf, wr, tm):
    t, d = x.shape
    row = pl.BlockSpec((tm, d), lambda i: (i, 0))
    nar = pl.BlockSpec((tm, LANES), lambda i: (i, 0))
    return pl.pallas_call(
        _s5_glu_router_kernel, grid=(t // tm,),
        in_specs=[row, row, row, _resident(gm.shape), _resident(dskip.shape), _resident(wglu.shape),
                  _resident(gf.shape), _resident(wr.shape)],
        out_specs=[row, row, nar, nar],
        out_shape=[jax.ShapeDtypeStruct((t, d), F32), jax.ShapeDtypeStruct((t, d), F32),
                   jax.ShapeDtypeStruct((t, LANES), jnp.int32), jax.ShapeDtypeStruct((t, LANES), F32)],
        compiler_params=_cparams("parallel"), name="s5_glu_router",
    )(x, yf, yb, gm, dskip, wglu, gf, wr)


def _moe_plan_kernel(idx_ref, ltri_ref, rank_ref, cnt_ref, carry_ref):
    @pl.when(pl.program_id(0) == 0)
    def _():
        carry_ref[...] = jnp.zeros_like(carry_ref)

    idx = idx_ref[...]
    lane = lax.broadcasted_iota(jnp.int32, idx.shape, 1)
    oh1 = lane == idx[:, 0:1]
    oh2 = lane == idx[:, 1:2]
    oh = jnp.where(oh1, 1.0, jnp.where(oh2, 1.0, 0.0)).astype(BF16)
    cs = _dot(ltri_ref[...], oh)
    tot = carry_ref[...] + cs - 1.0
    r1 = jnp.sum(jnp.where(oh1, tot, 0.0), axis=-1, keepdims=True)
    r2 = jnp.sum(jnp.where(oh2, tot, 0.0), axis=-1, keepdims=True)
    rank_ref[...] = jnp.where(lane == 0, r1, jnp.where(lane == 1, r2, 0.0)).astype(jnp.int32)
    new = carry_ref[...] + cs[cs.shape[0] - 1:, :]
    carry_ref[...] = new
    cnt_ref[...] = new


def _moe_plan(idx, tm):
    t = idx.shape[0]
    ltri = jnp.tril(jnp.ones((tm, tm), F32)).astype(BF16)
    nar = pl.BlockSpec((tm, LANES), lambda i: (i, 0))
    return pl.pallas_call(
        _moe_plan_kernel, grid=(t // tm,),
        in_specs=[nar, _resident(ltri.shape)],
        out_specs=[nar, pl.BlockSpec((1, LANES), lambda i: (0, 0))],
        out_shape=[jax.ShapeDtypeStruct((t, LANES), jnp.int32), jax.ShapeDtypeStruct((1, LANES), F32)],
        scratch_shapes=[pltpu.VMEM((1, LANES), F32)],
        compiler_params=_cparams("arbitrary"), name="moe_plan",
    )(idx, ltri)


ROW_UNROLL = 8


def _row_copy(src, dst, i, j, sem):
    return pltpu.make_async_copy(src.at[pl.ds(i, 1)], dst.at[pl.ds(j, 1)], sem)


def _moe_dispatch_kernel(pos_ref, x_ref, xs_in, xs_hbm, sem, *, tm):
    del xs_in

    def issue(g, c):
        for u in range(ROW_UNROLL):
            r = g * ROW_UNROLL + u
            _row_copy(x_ref, xs_hbm, r, pos_ref[0, 0, 2 * r], sem).start(priority=0)
            _row_copy(x_ref, xs_hbm, r, pos_ref[0, 0, 2 * r + 1], sem).start(priority=1)
        return c

    lax.fori_loop(0, tm // ROW_UNROLL, issue, 0)

    def drain(g, c):
        for _ in range(2 * ROW_UNROLL):
            _row_copy(x_ref, xs_hbm, 0, 0, sem).wait()
        return c

    lax.fori_loop(0, tm // ROW_UNROLL, drain, 0)


def _moe_dispatch(h2, pos, rows, tm):
    t, d = h2.shape
    pos3 = pos.reshape(t // tm, 1, 2 * tm)
    xs0 = jnp.zeros((rows, d), F32)
    return pl.pallas_call(
        functools.partial(_moe_dispatch_kernel, tm=tm), grid=(t // tm,),
        in_specs=[pl.BlockSpec((1, 1, 2 * tm), lambda i: (i, 0, 0), memory_space=pltpu.SMEM),
                  pl.BlockSpec((tm, d), lambda i: (i, 0)), pl.BlockSpec(memory_space=pl.ANY)],
        out_specs=pl.BlockSpec(memory_space=pl.ANY),
        out_shape=jax.ShapeDtypeStruct((rows, d), F32),
        scratch_shapes=[pltpu.SemaphoreType.DMA(())],
        input_output_aliases={2: 0},
        compiler_params=_cparams("arbitrary"), name="moe_dispatch",
    )(pos3, h2, xs0)


def _moe_expert_kernel(te_ref, nu_ref, xs_ref, wg_ref, wu_ref, wd_ref, y_ref, act_ref, *, fc):
    i = pl.program_id(0)

    @pl.when(i < nu_ref[0])
    def _():
        y_ref[...] = _swiglu(xs_ref[...].astype(BF16), wg_ref.at[0], wu_ref.at[0], wd_ref.at[0], act_ref, fc)

    @pl.when(i >= nu_ref[0])
    def _():
        y_ref[...] = jnp.zeros_like(y_ref)


def _moe_experts(xs, tile_expert, n_used, wg, wu, wd, tm):
    rows, d = xs.shape
    f = wg.shape[2]
    fc = _tile(f, 1792, LANES)
    wspec = lambda shp: pl.BlockSpec((1,) + shp, lambda i, te, nu: (te[i], 0, 0), pipeline_mode=pl.Buffered(1))
    gs = pltpu.PrefetchScalarGridSpec(
        num_scalar_prefetch=2, grid=(rows // tm,),
        in_specs=[pl.BlockSpec((tm, d), lambda i, te, nu: (i, 0)),
                  wspec((d, f)), wspec((d, f)), wspec((f, d))],
        out_specs=pl.BlockSpec((tm, d), lambda i, te, nu: (i, 0)),
        scratch_shapes=[pltpu.VMEM((tm, f), BF16)])
    return pl.pallas_call(
        functools.partial(_moe_expert_kernel, fc=fc), grid_spec=gs,
        out_shape=jax.ShapeDtypeStruct((rows, d), F32),
        compiler_params=_cparams("arbitrary"), name="moe_experts",
    )(tile_expert, n_used, xs, wg, wu, wd)


def _moe_combine_kernel(pos_ref, x_ref, gate_ref, ys_hbm, gf_ref, o_ref, b0_ref, b1_ref, sem, *, tm, final_norm):
    def issue(g, c):
        for u in range(ROW_UNROLL):
            r = g * ROW_UNROLL + u
            _row_copy(ys_hbm, b0_ref, pos_ref[0, 0, 2 * r], r, sem).start(priority=0)
            _row_copy(ys_hbm, b1_ref, pos_ref[0, 0, 2 * r + 1], r, sem).start(priority=1)
        return c

    lax.fori_loop(0, tm // ROW_UNROLL, issue, 0)

    def drain(g, c):
        for _ in range(ROW_UNROLL):
            _row_copy(ys_hbm, b0_ref, 0, 0, sem).wait()
            _row_copy(ys_hbm, b1_ref, 0, 0, sem).wait()
        return c

    lax.fori_loop(0, tm // ROW_UNROLL, drain, 0)
    gate = gate_ref[...]
    out = x_ref[...] + gate[:, 0:1] * b0_ref[...] + gate[:, 1:2] * b1_ref[...]
    if final_norm:
        out = _rms(out, gf_ref[...])
    o_ref[...] = out


def _moe_combine(x1, gate, ys, pos, gfin, tm, final_norm):
    t, d = x1.shape
    pos3 = pos.reshape(t // tm, 1, 2 * tm)
    return pl.pallas_call(
        functools.partial(_moe_combine_kernel, tm=tm, final_norm=final_norm), grid=(t // tm,),
        in_specs=[pl.BlockSpec((1, 1, 2 * tm), lambda i: (i, 0, 0), memory_space=pltpu.SMEM),
                  pl.BlockSpec((tm, d), lambda i: (i, 0)),
                  pl.BlockSpec((tm, LANES), lambda i: (i, 0)),
                  pl.BlockSpec(memory_space=pl.ANY), _resident(gfin.shape)],
        out_specs=pl.BlockSpec((tm, d), lambda i: (i, 0)),
        out_shape=jax.ShapeDtypeStruct((t, d), F32),
        scratch_shapes=[pltpu.VMEM((tm, d), F32), pltpu.VMEM((tm, d), F32), pltpu.SemaphoreType.DMA(())],
        compiler_params=_cparams("arbitrary"), name="moe_combine",
    )(pos3, x1, gate, ys, gfin)


def _rmsnorm_kernel(x_ref, g_ref, o_ref):
    o_ref[...] = _rms(x_ref[...], g_ref[...])


def _rmsnorm(x, g, tm):
    t, d = x.shape
    return pl.pallas_call(
        _rmsnorm_kernel, grid=(t // tm,),
        in_specs=[pl.BlockSpec((tm, d), lambda i: (i, 0)), _resident(g.shape)],
        out_specs=pl.BlockSpec((tm, d), lambda i: (i, 0)),
        out_shape=jax.ShapeDtypeStruct((t, d), F32),
        compiler_params=_cparams("parallel"), name="final_norm",
    )(x, g)


def _rot_cols(w):
    half = QK_ROPE // 2
    return jnp.concatenate([-w[..., half:], w[..., :half]], axis=-1)


def _prep_mla(w_dq, q_norm, w_uq, w_dkv, kv_norm, w_ukv, w_o):
    ql = w_dq.shape[1]
    kvl = w_ukv.shape[0]
    wq = w_uq.reshape(ql, N_HEADS, QK_NOPE + QK_ROPE)
    wq_nope, wq_rope = wq[..., :QK_NOPE], wq[..., QK_NOPE:]
    zpad = jnp.zeros((ql, N_HEADS, HEAD_PAD - QK_NOPE - QK_ROPE), F32)
    w1 = jnp.concatenate([wq_nope, wq_rope, zpad], axis=-1).reshape(ql, N_HEADS * HEAD_PAD)
    w2 = jnp.concatenate([jnp.zeros_like(wq_nope), _rot_cols(wq_rope), zpad], axis=-1).reshape(ql, N_HEADS * HEAD_PAD)
    kr_w = w_dkv[:, kvl:]
    lpad = jnp.zeros((w_dkv.shape[0], LANES - QK_ROPE), F32)
    wdkv = jnp.concatenate([w_dkv[:, :kvl], kr_w, lpad, _rot_cols(kr_w), lpad], axis=1)
    wkv = w_ukv.reshape(kvl, N_HEADS, QK_NOPE + V_DIM)
    wk_top = jnp.concatenate([wkv[..., :QK_NOPE], jnp.zeros((kvl, N_HEADS, HEAD_PAD - QK_NOPE), F32)], axis=-1)
    eye = jnp.eye(LANES, QK_ROPE, dtype=F32)
    wk_bot = jnp.concatenate([jnp.zeros((LANES, QK_NOPE), F32), eye,
                              jnp.zeros((LANES, HEAD_PAD - QK_NOPE - QK_ROPE), F32)], axis=-1)
    wk_bot = jnp.broadcast_to(wk_bot[:, None, :], (LANES, N_HEADS, HEAD_PAD))
    wk = jnp.concatenate([wk_top, wk_bot], axis=0).reshape(kvl + LANES, N_HEADS * HEAD_PAD)
    wvt = wkv[..., QK_NOPE:].reshape(kvl, N_HEADS * V_DIM).T
    return dict(wdq=w_dq.astype(BF16), qn=q_norm.reshape(1, -1), w1t=w1.T.astype(BF16), w2t=w2.T.astype(BF16),
                wdkv=wdkv.astype(BF16), kvn=kv_norm.reshape(1, -1), wk=wk.astype(BF16), wvt=wvt.astype(BF16),
                wo=w_o.astype(BF16))


def _rope_tables(s):
    pos = jnp.arange(s, dtype=F32)
    inv = ROPE_THETA ** (-jnp.arange(0, QK_ROPE, 2, dtype=F32) / QK_ROPE)
    ang = pos[:, None] * inv[None, :]
    ang = jnp.concatenate([ang, ang], axis=-1)
    cos, sin = jnp.cos(ang), jnp.sin(ang)
    qs = ATTN_SCALE * LOG2E
    zq = jnp.zeros((s, HEAD_PAD - QK_NOPE - QK_ROPE), F32)
    ctq = jnp.concatenate([jnp.full((s, QK_NOPE), qs, F32), cos * qs, zq], axis=1).T
    stq = jnp.concatenate([jnp.zeros((s, QK_NOPE), F32), sin * qs, zq], axis=1).T
    zk = jnp.zeros((s, LANES - QK_ROPE), F32)
    ctk = jnp.concatenate([cos, zk], axis=1)
    stk = jnp.concatenate([sin, zk], axis=1)
    return ctq, stq, ctk, stk


def _prep_s5(a_re, a_im, log_step, b_re, b_im, c_re, c_im):
    _, g, p = a_re.shape
    n = g * p * GROUP_CH
    rep = lambda a: jnp.broadcast_to(a[..., None], (2, g, p, GROUP_CH)).reshape(2, n)
    ls = jnp.broadcast_to(log_step[:, :, None, None], (2, g, p, GROUP_CH)).reshape(2, n)
    outs = pl.pallas_call(
        _s5_param_kernel,
        out_shape=[jax.ShapeDtypeStruct((2, n), F32)] * 4, name="s5_params",
    )(rep(a_re), rep(a_im), ls, b_re.reshape(2, n), b_im.reshape(2, n))
    lr, li, bbr, bbi = [o.reshape(2, g, p, GROUP_CH) for o in outs]
    lr, li = lr[..., 0], li[..., 0]
    gpt = MXU_DIM // GROUP_CH
    ntile = g // gpt
    eye = jnp.eye(gpt, dtype=F32)

    def tiles_b(bb):
        t = bb.reshape(2, ntile, gpt, p, GROUP_CH)
        return jnp.einsum('djapc,ab->djacbp', t, eye).reshape(2, ntile, gpt * GROUP_CH, gpt * p)

    def tiles_c(cc):
        t = cc.reshape(2, ntile, gpt, GROUP_CH, p)
        return jnp.einsum('djacp,ab->djapbc', t, eye).reshape(2, ntile, gpt * p, gpt * GROUP_CH)

    bt = jnp.concatenate([tiles_b(bbr), tiles_b(bbi)], axis=-1).astype(BF16)
    ct = jnp.concatenate([tiles_c(c_re), tiles_c(-c_im)], axis=-2).astype(BF16)
    lrt = lr.reshape(2, ntile, 1, gpt * p)
    lit = li.reshape(2, ntile, 1, gpt * p)
    return bt, ct, lrt, lit


def _split_bf16(w):
    hi = w.astype(BF16)
    lo = (w - hi.astype(F32)).astype(BF16)
    return jnp.stack([hi, lo])


def _cpow(lr, li, n):
    k = int(math.log2(n))
    assert 2 ** k == n
    for _ in range(k):
        lr, li = lr * lr - li * li, 2.0 * lr * li
    return lr, li


def _mla_layer(x, p, j, tabs, tm, tq, tk, unroll):
    w = p['mla'][j]
    qt, k, vt = _mla_pre(x, p['norm_mix'][2 * j], w, tabs, tm)
    ot = _attention(qt, k, vt, tq, tk, unroll)
    f = p['ffn'][j]
    return _attn_out_ffn(x, ot, w['wo'], p['norm_ffn'][2 * j], f['wg'], f['wu'], f['wd'], tm)


def _s5_states(xp, g, s5, nseg, tt):
    bt, ct, lr, li = s5
    nrg = xp.shape[0]
    ntile, half = bt.shape[1], bt.shape[3] // 2
    zero = jnp.zeros((nrg, ntile, 2, SUBLANES, half), F32)
    if nseg == 1:
        return zero, zero
    seg_len = xp.shape[1] // SUBLANES
    inits = []
    for k, reverse in ((0, False), (1, True)):
        _, fin = _s5_scan(xp, g, bt[k], ct[k], lr[k], li[k], zero, tt, reverse, False)
        e = fin.reshape(nrg, ntile, 2, SUBLANES // nseg, nseg, half)
        plr, pli = _cpow(lr[k], li[k], seg_len)
        plr, pli = plr[None], pli[None]
        order = range(nseg - 1, -1, -1) if reverse else range(nseg)
        sr = jnp.zeros_like(e[:, :, 0, :, 0])
        si = jnp.zeros_like(sr)
        init_r = [None] * nseg
        init_i = [None] * nseg
        for q in order:
            init_r[q], init_i[q] = sr, si
            er, ei = e[:, :, 0, :, q], e[:, :, 1, :, q]
            sr, si = plr * sr - pli * si + er, plr * si + pli * sr + ei
        init = jnp.stack([jnp.stack(init_r, axis=3), jnp.stack(init_i, axis=3)], axis=2)
        inits.append(init.reshape(nrg, ntile, 2, SUBLANES, half))
    return inits[0], inits[1]


def _s5_moe_layer(xp, p, j, nseg, tt, tm, last):
    nrg, rows, d = xp.shape
    i = 2 * j + 1
    s5 = p['s5'][j]
    bt, ct, lr, li = s5
    gm = p['norm_mix'][i]
    init_f, init_b = _s5_states(xp, gm, s5, nseg, tt)
    yf, _ = _s5_scan(xp, gm, bt[0], ct[0], lr[0], li[0], init_f, tt, False, True)
    yb, _ = _s5_scan(xp, gm, bt[1], ct[1], lr[1], li[1], init_b, tt, True, True)
    t = nrg * rows
    m = p['moe'][j]
    x1, h2, idx, gate = _s5_glu_router(xp.reshape(t, d), yf.reshape(t, d), yb.reshape(t, d), gm,
                                       p['s5_d'][j], p['s5_wglu'][j], p['norm_ffn'][i], m['wr'], tm)
    rank, cnt = _moe_plan(idx, tm)
    counts = cnt[0, :N_EXPERTS].astype(jnp.int32)
    ntiles = (counts + tm - 1) // tm
    tile_end = jnp.cumsum(ntiles)
    offs = (tile_end - ntiles) * tm
    e2 = idx[:, :2]
    pos = offs[e2] + rank[:, :2]
    nt_max = (2 * t) // tm + N_EXPERTS
    tile_expert = jnp.minimum(jnp.searchsorted(tile_end, jnp.arange(nt_max, dtype=jnp.int32), side='right'),
                              N_EXPERTS - 1).astype(jnp.int32)
    n_used = tile_end[-1:].astype(jnp.int32)
    xs = _moe_dispatch(h2, pos, nt_max * tm, tm)
    ys = _moe_experts(xs, tile_expert, n_used, m['wg'], m['wu'], m['wd'], tm)
    out = _moe_combine(x1, gate, ys, pos, p['norm_final'], tm, last)
    return out.reshape(nrg, rows, d)


def _to_perm(x, nseg):
    b, s, d = x.shape
    nrg = b * nseg // SUBLANES
    xr = x.reshape(nrg, SUBLANES, s // nseg, d)
    return jnp.swapaxes(xr, 1, 2).reshape(nrg, (s // nseg) * SUBLANES, d)


def _from_perm(xp, b, s, nseg):
    nrg, rows, d = xp.shape
    xr = xp.reshape(nrg, rows // SUBLANES, SUBLANES, d)
    return jnp.swapaxes(xr, 1, 2).reshape(b, s, d)


def _trunk(x, p):
    b, s, d = x.shape
    tm = _tile(s, 512)
    long_seq = s > 4096
    tq = _tile(s, 512 if long_seq else 2048, LANES)
    unroll = 4 if long_seq else 2
    tk = _tile(s, min(512, s // 4), LANES)
    nseg = 1 if b % SUBLANES == 0 else SUBLANES // b
    assert (b * nseg) % SUBLANES == 0 and s % nseg == 0
    tt = _tile(s // nseg, 64)
    tabs = _rope_tables(s)
    depth = p['norm_mix'].shape[0]
    for i in range(depth):
        j = i // 2
        if i % 2 == 0:
            x = _mla_layer(x, p, j, tabs, tm, tq, tk, unroll)
        else:
            xp = _s5_moe_layer(_to_perm(x, nseg), p, j, nseg, tt, tm, last=(i == depth - 1))
            x = _from_perm(xp, b, s, nseg)
    if depth % 2 == 1:
        x = _rmsnorm(x.reshape(b * s, d), p['norm_final'], tm).reshape(b, s, d)
    return x


def kernel(x_prompt, x_sample, norm_mix, norm_ffn, norm_final, mla_w_dq, mla_q_norm, mla_w_uq, mla_w_dkv,
           mla_kv_norm, mla_w_ukv, mla_w_o, ssm_a_re, ssm_a_im, ssm_log_step, ssm_b_re, ssm_b_im, ssm_c_re,
           ssm_c_im, ssm_d, ssm_w_glu, ffn_w_gate, ffn_w_up, ffn_w_down, moe_w_router, moe_w_gate, moe_w_up,
           moe_w_down):
    d = x_prompt.shape[-1]
    na, ns = mla_w_dq.shape[0], ssm_a_re.shape[0]
    p = dict(
        norm_mix=norm_mix.reshape(-1, 1, d), norm_ffn=norm_ffn.reshape(-1, 1, d), norm_final=norm_final.reshape(1, d),
        mla=[_prep_mla(mla_w_dq[j], mla_q_norm[j], mla_w_uq[j], mla_w_dkv[j], mla_kv_norm[j], mla_w_ukv[j],
                       mla_w_o[j]) for j in range(na)],
        ffn=[dict(wg=ffn_w_gate[j].astype(BF16), wu=ffn_w_up[j].astype(BF16), wd=ffn_w_down[j].astype(BF16))
             for j in range(na)],
        s5=[_prep_s5(ssm_a_re[j], ssm_a_im[j], ssm_log_step[j], ssm_b_re[j], ssm_b_im[j], ssm_c_re[j], ssm_c_im[j])
            for j in range(ns)],
        s5_d=[ssm_d[j].reshape(1, d) for j in range(ns)],
        s5_wglu=[ssm_w_glu[j].astype(BF16) for j in range(ns)],
        moe=[dict(wr=_split_bf16(jnp.pad(moe_w_router[j], ((0, 0), (0, LANES - N_EXPERTS)))),
                  wg=moe_w_gate[j].astype(BF16), wu=moe_w_up[j].astype(BF16), wd=moe_w_down[j].astype(BF16))
             for j in range(ns)],
    )
    return (_trunk(x_prompt, p), _trunk(x_sample, p))
```

```python
import functools
import math

import jax
import jax.numpy as jnp
from jax import lax
from jax.experimental import pallas as pl
from jax.experimental.pallas import tpu as pltpu

BF16 = jnp.bfloat16
F32 = jnp.float32

N_HEADS = 16
QK_NOPE = 64
QK_ROPE = 32
V_DIM = 64
HEAD_PAD = 128
ROPE_THETA = 10000.0
ATTN_SCALE = (QK_NOPE + QK_ROPE) ** -0.5
LOG2E = 1.4426950408889634
GROUP_CH = 16
STATE_DIM = 64
N_EXPERTS = 8
NORM_EPS = 1e-6
LANES = 128
SUBLANES = 8
MXU_DIM = 256
VMEM_LIMIT = 56 * 1024 * 1024
NEG_BIG = -1e30


def _cparams(*sem):
    return pltpu.CompilerParams(dimension_semantics=sem, vmem_limit_bytes=VMEM_LIMIT)


def _resident(shape):
    zeros = (0,) * len(shape)
    return pl.BlockSpec(shape, lambda *_: zeros, pipeline_mode=pl.Buffered(1))


def _tile(n, cap, mult=SUBLANES):
    if n <= cap:
        return n
    t = (cap // mult) * mult
    while t > mult and n % t:
        t -= mult
    assert n % t == 0, (n, cap, mult)
    return t


def _rms(x, g):
    ms = jnp.mean(x * x, axis=-1, keepdims=True)
    return x * lax.rsqrt(ms + NORM_EPS) * g


def _dot(a, b):
    return jnp.dot(a, b, preferred_element_type=F32)


def _dot_nt(a, b):
    return lax.dot_general(a, b, (((1,), (1,)), ((), ())), preferred_element_type=F32)


def _dot_tn(a, b):
    return lax.dot_general(a, b, (((0,), (0,)), ((), ())), preferred_element_type=F32)


def _swiglu(h, wg_ref, wu_ref, wd_ref, act_ref, fc):
    f = act_ref.shape[1]
    for c0 in range(0, f, fc):
        g = _dot(h, wg_ref[:, c0:c0 + fc])
        u = _dot(h, wu_ref[:, c0:c0 + fc])
        act_ref[:, c0:c0 + fc] = (g * jax.nn.sigmoid(g) * u).astype(BF16)
    return _dot(act_ref[...], wd_ref[...])


def _mla_pre_kernel(x_ref, g_ref, wdq_ref, qn_ref, w1_ref, w2_ref, wdkv_ref, kvn_ref, wk_ref, wv_ref,
                    ctq_ref, stq_ref, ctk_ref, stk_ref, qt_ref, k_ref, vt_ref):
    h = _rms(x_ref[0], g_ref[...]).astype(BF16)
    cq = _rms(_dot(h, wdq_ref[...]), qn_ref[...]).astype(BF16)
    a = _dot_nt(w1_ref[...], cq)
    b = _dot_nt(w2_ref[...], cq)
    ct = ctq_ref[...]
    st = stq_ref[...]
    for hh in range(N_HEADS):
        sl = slice(hh * HEAD_PAD, (hh + 1) * HEAD_PAD)
        qt_ref[0, sl, :] = (a[sl] * ct + b[sl] * st).astype(BF16)
    ck = _dot(h, wdkv_ref[...])
    kvl = kvn_ref.shape[1]
    ckv = _rms(ck[:, :kvl], kvn_ref[...]).astype(BF16)
    kr = (ck[:, kvl:kvl + LANES] * ctk_ref[...] + ck[:, kvl + LANES:] * stk_ref[...]).astype(BF16)
    kaug = jnp.concatenate([ckv, kr], axis=1)
    k_ref[0] = _dot(kaug, wk_ref[...]).astype(BF16)
    vt_ref[0] = _dot_nt(wv_ref[...], ckv).astype(BF16)


def _mla_pre(x, g, w, tabs, tm):
    b, s, d = x.shape
    hq = N_HEADS * HEAD_PAD
    hv = N_HEADS * V_DIM
    ctq, stq, ctk, stk = tabs
    grid = (b, s // tm)
    in_specs = [
        pl.BlockSpec((1, tm, d), lambda bi, i: (bi, i, 0)),
        _resident(g.shape), _resident(w['wdq'].shape), _resident(w['qn'].shape),
        _resident(w['w1t'].shape), _resident(w['w2t'].shape), _resident(w['wdkv'].shape),
        _resident(w['kvn'].shape), _resident(w['wk'].shape), _resident(w['wvt'].shape),
        pl.BlockSpec((HEAD_PAD, tm), lambda bi, i: (0, i)),
        pl.BlockSpec((HEAD_PAD, tm), lambda bi, i: (0, i)),
        pl.BlockSpec((tm, LANES), lambda bi, i: (i, 0)),
        pl.BlockSpec((tm, LANES), lambda bi, i: (i, 0)),
    ]
    out_specs = [
        pl.BlockSpec((1, hq, tm), lambda bi, i: (bi, 0, i)),
        pl.BlockSpec((1, tm, hq), lambda bi, i: (bi, i, 0)),
        pl.BlockSpec((1, hv, tm), lambda bi, i: (bi, 0, i)),
    ]
    out_shape = [jax.ShapeDtypeStruct((b, hq, s), BF16), jax.ShapeDtypeStruct((b, s, hq), BF16),
                 jax.ShapeDtypeStruct((b, hv, s), BF16)]
    return pl.pallas_call(
        _mla_pre_kernel, grid=grid, in_specs=in_specs, out_specs=out_specs, out_shape=out_shape,
        compiler_params=_cparams("parallel", "parallel"), name="mla_pre",
    )(x, g, w['wdq'], w['qn'], w['w1t'], w['w2t'], w['wdkv'], w['kvn'], w['wk'], w['wvt'], ctq, stq, ctk, stk)


SUM_ROWS = 16


def _attn_kernel(qt_ref, k_ref, vt_ref, o_ref, sa_ref, sb_ref, acc_ref, m_ref, ma_ref, mb_ref, *, tk, unroll):
    qt = qt_ref[0]
    nk = k_ref.shape[1] // tk
    ones = jnp.ones((SUM_ROWS, tk), BF16)

    def scores(j, dst_ref):
        k0 = pl.multiple_of(j * tk, tk)
        s = _dot(k_ref[0, pl.ds(k0, tk), :], qt)
        dst_ref[...] = s
        bmax_ref = ma_ref if dst_ref is sa_ref else mb_ref
        bmax_ref[...] = jnp.max(s, axis=0, keepdims=True)

    def update(j, src_ref):
        k0 = pl.multiple_of(j * tk, tk)
        s = src_ref[...]
        m_old = m_ref[...]
        m_new = jnp.maximum(m_old, (ma_ref if src_ref is sa_ref else mb_ref)[...])
        p = jnp.exp2(s - m_new).astype(BF16)
        v = jnp.concatenate([vt_ref[0, :, pl.ds(k0, tk)], ones], axis=0)
        acc_ref[...] = jnp.exp2(m_old - m_new) * acc_ref[...] + _dot(v, p)
        m_ref[...] = m_new

    m_ref[...] = jnp.full(m_ref.shape, NEG_BIG, F32)
    acc_ref[...] = jnp.zeros(acc_ref.shape, F32)
    scores(0, sa_ref)

    def pair(i, c):
        j = 2 * i
        scores(j + 1, sb_ref)
        update(j, sa_ref)
        scores(j + 2, sa_ref)
        update(j + 1, sb_ref)
        return c

    lax.fori_loop(0, nk // 2 - 1, pair, 0, unroll=unroll)
    scores(nk - 1, sb_ref)
    update(nk - 2, sa_ref)
    update(nk - 1, sb_ref)
    acc = acc_ref[...]
    o_ref[0] = (acc[:V_DIM] / acc[V_DIM:V_DIM + 1]).astype(BF16)


def _attention(qt, k, vt, tq, tk, unroll):
    b, hq, s = qt.shape
    assert (s // tk) % 2 == 0
    grid = (b, N_HEADS, s // tq)
    return pl.pallas_call(
        functools.partial(_attn_kernel, tk=tk, unroll=unroll), grid=grid,
        in_specs=[pl.BlockSpec((1, HEAD_PAD, tq), lambda bi, h, i: (bi, h, i)),
                  pl.BlockSpec((1, s, HEAD_PAD), lambda bi, h, i: (bi, 0, h)),
                  pl.BlockSpec((1, V_DIM, s), lambda bi, h, i: (bi, h, 0))],
        out_specs=pl.BlockSpec((1, V_DIM, tq), lambda bi, h, i: (bi, h, i)),
        out_shape=jax.ShapeDtypeStruct((b, N_HEADS * V_DIM, s), BF16),
        scratch_shapes=[pltpu.VMEM((tk, tq), F32), pltpu.VMEM((tk, tq), F32),
                        pltpu.VMEM((V_DIM + SUM_ROWS, tq), F32), pltpu.VMEM((1, tq), F32),
                        pltpu.VMEM((1, tq), F32), pltpu.VMEM((1, tq), F32)],
        compiler_params=_cparams("parallel", "parallel", "parallel"), name="mla_attention",
    )(qt, k, vt)


def _attn_out_ffn_kernel(x_ref, ot_ref, wo_ref, g_ref, wg_ref, wu_ref, wd_ref, y_ref, act_ref, *, fc):
    x1 = x_ref[0] + _dot_tn(ot_ref[0], wo_ref[...])
    h = _rms(x1, g_ref[...]).astype(BF16)
    y_ref[0] = x1 + _swiglu(h, wg_ref, wu_ref, wd_ref, act_ref, fc)


def _attn_out_ffn(x, ot, wo, g, wg, wu, wd, tm):
    b, s, d = x.shape
    f = wg.shape[1]
    fc = _tile(f, 1536, LANES)
    hv = ot.shape[1]
    return pl.pallas_call(
        functools.partial(_attn_out_ffn_kernel, fc=fc), grid=(b, s // tm),
        in_specs=[pl.BlockSpec((1, tm, d), lambda bi, i: (bi, i, 0)),
                  pl.BlockSpec((1, hv, tm), lambda bi, i: (bi, 0, i)),
                  _resident(wo.shape), _resident(g.shape), _resident(wg.shape), _resident(wu.shape),
                  _resident(wd.shape)],
        out_specs=pl.BlockSpec((1, tm, d), lambda bi, i: (bi, i, 0)),
        out_shape=jax.ShapeDtypeStruct((b, s, d), F32),
        scratch_shapes=[pltpu.VMEM((tm, f), BF16)],
        compiler_params=_cparams("parallel", "parallel"), name="attn_out_ffn",
    )(x, ot, wo, g, wg, wu, wd)


def _s5_param_kernel(are_ref, aim_ref, ls_ref, bre_ref, bim_ref, lr_ref, li_ref, bbr_ref, bbi_ref):
    ar = are_ref[...]
    ai = aim_ref[...]
    delta = jnp.exp(ls_ref[...])
    mag = jnp.exp(ar * delta)
    lr = mag * jnp.cos(ai * delta)
    li = mag * jnp.sin(ai * delta)
    nr = lr - 1.0
    den = ar * ar + ai * ai
    cr = (nr * ar + li * ai) / den
    ci = (li * ar - nr * ai) / den
    br = bre_ref[...]
    bi = bim_ref[...]
    lr_ref[...] = lr
    li_ref[...] = li
    bbr_ref[...] = cr * br - ci * bi
    bbi_ref[...] = cr * bi + ci * br


def _s5_scan_kernel(x_ref, g_ref, bt_ref, ct_ref, lr_ref, li_ref, init_ref, *rest, tt, reverse, emit_y):
    if emit_y:
        y_ref, fin_ref, bu0_ref, bu1_ref, st_ref = rest
    else:
        fin_ref, bu0_ref, bu1_ref, st_ref = rest
    c = pl.program_id(1)

    @pl.when(c == 0)
    def _():
        st_ref[...] = init_ref[0]

    hb = _rms(x_ref[0], g_ref[...]).astype(BF16)

    def write_y(j, y):
        y_ref[0, :, j * MXU_DIM:(j + 1) * MXU_DIM] = y

    _s5_direction(hb, bt_ref, ct_ref, lr_ref, li_ref, st_ref, (bu0_ref, bu1_ref), tt, reverse,
                  write_y if emit_y else None)

    @pl.when(c == pl.num_programs(1) - 1)
    def _():
        fin_ref[0] = st_ref[...]


def _s5_direction(hb, bt_ref, ct_ref, lr_ref, li_ref, st_ref, bu_refs, tt, reverse, write_y):
    ntile = bt_ref.shape[0]
    half = bt_ref.shape[2] // 2
    for j in range(ntile):
        bu_ref = bu_refs[j % 2]
        bu_ref[...] = _dot(hb[:, j * MXU_DIM:(j + 1) * MXU_DIM], bt_ref[j])
        lr = jnp.broadcast_to(lr_ref[j], (SUBLANES, half))
        li = jnp.broadcast_to(li_ref[j], (SUBLANES, half))
        xr, xi = st_ref[j, 0], st_ref[j, 1]
        for t in range(tt):
            r0 = ((tt - 1 - t) if reverse else t) * SUBLANES
            nr = lr * xr - li * xi + bu_ref[r0:r0 + SUBLANES, 0:half]
            ni = lr * xi + li * xr + bu_ref[r0:r0 + SUBLANES, half:2 * half]
            bu_ref[r0:r0 + SUBLANES, 0:half] = nr
            bu_ref[r0:r0 + SUBLANES, half:2 * half] = ni
            xr, xi = nr, ni
        st_ref[j, 0] = xr
        st_ref[j, 1] = xi
        if write_y is not None:
            write_y(j, _dot(bu_ref[...].astype(BF16), ct_ref[j]))


def _s5_glu_router_kernel(x_ref, yf_ref, yb_ref, gm_ref, d_ref, wglu_ref, gf_ref, wr_ref,
                          x1_ref, h2_ref, idx_ref, gate_ref):
    tm = x_ref.shape[0]
    sub = tm // 2 if tm % (2 * SUBLANES) == 0 else tm
    for r0 in range(0, tm, sub):
        rows = slice(r0, r0 + sub)
        x1, h2, idx, gate = _glu_router_rows(x_ref[rows, :], yf_ref[rows, :] + yb_ref[rows, :], gm_ref, d_ref,
                                             wglu_ref, gf_ref, wr_ref)
        x1_ref[rows, :] = x1
        h2_ref[rows, :] = h2
        idx_ref[rows, :] = idx
        gate_ref[rows, :] = gate


def _glu_router_rows(x, ysum, gm_ref, d_ref, wglu_ref, gf_ref, wr_ref):
    d = x.shape[1]
    y = ysum + d_ref[...] * _rms(x, gm_ref[...])
    g = jax.nn.gelu(y, approximate=True).astype(BF16)
    z = _dot(g, wglu_ref[...])
    x1 = x + z[:, :d] * jax.nn.sigmoid(z[:, d:])
    h2 = _rms(x1, gf_ref[...])
    hi = h2.astype(BF16)
    lo = (h2 - hi.astype(F32)).astype(BF16)
    logits = _dot(hi, wr_ref[0]) + (_dot(lo, wr_ref[0]) + _dot(hi, wr_ref[1]))
    idx, gate = _topk2(logits)
    return x1, h2, idx, gate


def _s5_glu_router(x, yf, yb, gm, dskip, wglu, gf, wr, tm):
    t, d = x.shape
    row = pl.BlockSpec((tm, d), lambda i: (i, 0))
    nar = pl.BlockSpec((tm, LANES), lambda i: (i, 0))
    return pl.pallas_call(
        _s5_glu_router_kernel, grid=(t // tm,),
        in_specs=[row, row, row, _resident(gm.shape), _resident(dskip.shape), _resident(wglu.shape),
                  _resident(gf.shape), _resident(wr.shape)],
        out_specs=[row, row, nar, nar],
        out_shape=[jax.ShapeDtypeStruct((t, d), F32), jax.ShapeDtypeStruct((t, d), F32),
                   jax.ShapeDtypeStruct((t, LANES), jnp.int32), jax.ShapeDtypeStruct((t, LANES), F32)],
        compiler_params=_cparams("parallel"), name="s5_glu_router",
    )(x, yf, yb, gm, dskip, wglu, gf, wr)


def _s5_scan(xp, g, bt, ct, lr, li, init, tt, reverse, emit_y):
    nrg, rows, d = xp.shape
    s = rows // SUBLANES
    nc = s // tt
    ntile, _, sw = bt.shape
    half = sw // 2
    blk = tt * SUBLANES
    if reverse:
        xmap = lambda gi, c: (gi, nc - 1 - c, 0)
    else:
        xmap = lambda gi, c: (gi, c, 0)
    st_shape = (ntile, 2, SUBLANES, half)
    fin_spec = pl.BlockSpec((1,) + st_shape, lambda gi, c: (gi, 0, 0, 0, 0))
    fin_shape = jax.ShapeDtypeStruct((nrg,) + st_shape, F32)
    if emit_y:
        out_specs = [pl.BlockSpec((1, blk, d), xmap), fin_spec]
        out_shape = [jax.ShapeDtypeStruct(xp.shape, F32), fin_shape]
    else:
        out_specs = [fin_spec]
        out_shape = [fin_shape]
    outs = pl.pallas_call(
        functools.partial(_s5_scan_kernel, tt=tt, reverse=reverse, emit_y=emit_y), grid=(nrg, nc),
        in_specs=[pl.BlockSpec((1, blk, d), xmap), _resident(g.shape), _resident(bt.shape),
                  _resident(ct.shape), _resident(lr.shape), _resident(li.shape),
                  pl.BlockSpec((1,) + st_shape, lambda gi, c: (gi, 0, 0, 0, 0))],
        out_specs=out_specs, out_shape=out_shape,
        scratch_shapes=[pltpu.VMEM((blk, sw), F32), pltpu.VMEM((blk, sw), F32), pltpu.VMEM(st_shape, F32)],
        compiler_params=_cparams("parallel", "arbitrary"),
        name="s5_scan_" + ("bwd" if reverse else "fwd") + ("" if emit_y else "_state"),
    )(xp, g, bt, ct, lr, li, init)
    if emit_y:
        return outs[0], outs[1]
    return None, outs[0]


def _topk2(logits):
    lane = lax.broadcasted_iota(jnp.int32, logits.shape, 1)
    lg = jnp.where(lane < N_EXPERTS, logits, -jnp.inf)
    m1 = jnp.max(lg, axis=-1, keepdims=True)
    i1 = jnp.min(jnp.where(lg == m1, lane, LANES), axis=-1, keepdims=True)
    lg2 = jnp.where(lane == i1, -jnp.inf, lg)
    m2 = jnp.max(lg2, axis=-1, keepdims=True)
    i2 = jnp.min(jnp.where(lg2 == m2, lane, LANES), axis=-1, keepdims=True)
    e = jnp.exp(m2 - m1)
    g1 = 1.0 / (1.0 + e)
    g2 = e / (1.0 + e)
    idx = jnp.where(lane == 0, i1, jnp.where(lane == 1, i2, 0))
    gate = jnp.where(lane == 0, g1, jnp.where(lane == 1, g2, 0.0))
    return idx, gate


def _moe_plan_kernel(idx_ref, ltri_ref, rank_ref, cnt_ref, carry_ref):
    @pl.when(pl.program_id(0) == 0)
    def _():
        carry_ref[...] = jnp.zeros_like(carry_ref)

    idx = idx_ref[...]
    lane = lax.broadcasted_iota(jnp.int32, idx.shape, 1)
    oh1 = lane == idx[:, 0:1]
    oh2 = lane == idx[:, 1:2]
    oh = jnp.where(oh1, 1.0, jnp.where(oh2, 1.0, 0.0)).astype(BF16)
    cs = _dot(ltri_ref[...], oh)
    tot = carry_ref[...] + cs - 1.0
    r1 = jnp.sum(jnp.where(oh1, tot, 0.0), axis=-1, keepdims=True)
    r2 = jnp.sum(jnp.where(oh2, tot, 0.0), axis=-1, keepdims=True)
    rank_ref[...] = jnp.where(lane == 0, r1, jnp.where(lane == 1, r2, 0.0)).astype(jnp.int32)
    new = carry_ref[...] + cs[cs.shape[0] - 1:, :]
    carry_ref[...] = new
    cnt_ref[...] = new


def _moe_plan(idx, tm):
    t = idx.shape[0]
    ltri = jnp.tril(jnp.ones((tm, tm), F32)).astype(BF16)
    nar = pl.BlockSpec((tm, LANES), lambda i: (i, 0))
    return pl.pallas_call(
        _moe_plan_kernel, grid=(t // tm,),
        in_specs=[nar, _resident(ltri.shape)],
        out_specs=[nar, pl.BlockSpec((1, LANES), lambda i: (0, 0))],
        out_shape=[jax.ShapeDtypeStruct((t, LANES), jnp.int32), jax.ShapeDtypeStruct((1, LANES), F32)],
        scratch_shapes=[pltpu.VMEM((1, LANES), F32)],
        compiler_params=_cparams("arbitrary"), name="moe_plan",
    )(idx, ltri)


ROW_UNROLL = 8


def _row_copy(src, dst, i, j, sem):
    return pltpu.make_async_copy(src.at[pl.ds(i, 1)], dst.at[pl.ds(j, 1)], sem)


def _moe_dispatch_kernel(pos_ref, x_ref, xs_in, xs_hbm, sem, *, tm):
    del xs_in

    def issue(g, c):
        for u in range(ROW_UNROLL):
            r = g * ROW_UNROLL + u
            _row_copy(x_ref, xs_hbm, r, pos_ref[0, 0, 2 * r], sem).start(priority=0)
            _row_copy(x_ref, xs_hbm, r, pos_ref[0, 0, 2 * r + 1], sem).start(priority=1)
        return c

    lax.fori_loop(0, tm // ROW_UNROLL, issue, 0)

    def drain(g, c):
        for _ in range(2 * ROW_UNROLL):
            _row_copy(x_ref, xs_hbm, 0, 0, sem).wait()
        return c

    lax.fori_loop(0, tm // ROW_UNROLL, drain, 0)


def _moe_dispatch(h2, pos, rows, tm):
    t, d = h2.shape
    pos3 = pos.reshape(t // tm, 1, 2 * tm)
    xs0 = jnp.zeros((rows, d), F32)
    return pl.pallas_call(
        functools.partial(_moe_dispatch_kernel, tm=tm), grid=(t // tm,),
        in_specs=[pl.BlockSpec((1, 1, 2 * tm), lambda i: (i, 0, 0), memory_space=pltpu.SMEM),
                  pl.BlockSpec((tm, d), lambda i: (i, 0)), pl.BlockSpec(memory_space=pl.ANY)],
        out_specs=pl.BlockSpec(memory_space=pl.ANY),
        out_shape=jax.ShapeDtypeStruct((rows, d), F32),
        scratch_shapes=[pltpu.SemaphoreType.DMA(())],
        input_output_aliases={2: 0},
        compiler_params=_cparams("arbitrary"), name="moe_dispatch",
    )(pos3, h2, xs0)


def _moe_expert_kernel(te_ref, nu_ref, xs_ref, wg_ref, wu_ref, wd_ref, y_ref, act_ref, *, fc):
    i = pl.program_id(0)

    @pl.when(i < nu_ref[0])
    def _():
        y_ref[...] = _swiglu(xs_ref[...].astype(BF16), wg_ref.at[0], wu_ref.at[0], wd_ref.at[0], act_ref, fc)

    @pl.when(i >= nu_ref[0])
    def _():
        y_ref[...] = jnp.zeros_like(y_ref)


def _moe_experts(xs, tile_expert, n_used, wg, wu, wd, tm):
    rows, d = xs.shape
    f = wg.shape[2]
    fc = _tile(f, 1792, LANES)
    wspec = lambda shp: pl.BlockSpec((1,) + shp, lambda i, te, nu: (te[i], 0, 0), pipeline_mode=pl.Buffered(1))
    gs = pltpu.PrefetchScalarGridSpec(
        num_scalar_prefetch=2, grid=(rows // tm,),
        in_specs=[pl.BlockSpec((tm, d), lambda i, te, nu: (i, 0)),
                  wspec((d, f)), wspec((d, f)), wspec((f, d))],
        out_specs=pl.BlockSpec((tm, d), lambda i, te, nu: (i, 0)),
        scratch_shapes=[pltpu.VMEM((tm, f), BF16)])
    return pl.pallas_call(
        functools.partial(_moe_expert_kernel, fc=fc), grid_spec=gs,
        out_shape=jax.ShapeDtypeStruct((rows, d), F32),
        compiler_params=_cparams("arbitrary"), name="moe_experts",
    )(tile_expert, n_used, xs, wg, wu, wd)


def _moe_combine_kernel(pos_ref, x_ref, gate_ref, ys_hbm, gf_ref, o_ref, b0_ref, b1_ref, sem, *, tm, final_norm):
    def issue(g, c):
        for u in range(ROW_UNROLL):
            r = g * ROW_UNROLL + u
            _row_copy(ys_hbm, b0_ref, pos_ref[0, 0, 2 * r], r, sem).start(priority=0)
            _row_copy(ys_hbm, b1_ref, pos_ref[0, 0, 2 * r + 1], r, sem).start(priority=1)
        return c

    lax.fori_loop(0, tm // ROW_UNROLL, issue, 0)

    def drain(g, c):
        for _ in range(ROW_UNROLL):
            _row_copy(ys_hbm, b0_ref, 0, 0, sem).wait()
            _row_copy(ys_hbm, b1_ref, 0, 0, sem).wait()
        return c

    lax.fori_loop(0, tm // ROW_UNROLL, drain, 0)
    gate = gate_ref[...]
    out = x_ref[...] + gate[:, 0:1] * b0_ref[...] + gate[:, 1:2] * b1_ref[...]
    if final_norm:
        out = _rms(out, gf_ref[...])
    o_ref[...] = out


def _moe_combine(x1, gate, ys, pos, gfin, tm, final_norm):
    t, d = x1.shape
    pos3 = pos.reshape(t // tm, 1, 2 * tm)
    return pl.pallas_call(
        functools.partial(_moe_combine_kernel, tm=tm, final_norm=final_norm), grid=(t // tm,),
        in_specs=[pl.BlockSpec((1, 1, 2 * tm), lambda i: (i, 0, 0), memory_space=pltpu.SMEM),
                  pl.BlockSpec((tm, d), lambda i: (i, 0)),
                  pl.BlockSpec((tm, LANES), lambda i: (i, 0)),
                  pl.BlockSpec(memory_space=pl.ANY), _resident(gfin.shape)],
        out_specs=pl.BlockSpec((tm, d), lambda i: (i, 0)),
        out_shape=jax.ShapeDtypeStruct((t, d), F32),
        scratch_shapes=[pltpu.VMEM((tm, d), F32), pltpu.VMEM((tm, d), F32), pltpu.SemaphoreType.DMA(())],
        compiler_params=_cparams("arbitrary"), name="moe_combine",
    )(pos3, x1, gate, ys, gfin)


def _rmsnorm_kernel(x_ref, g_ref, o_ref):
    o_ref[...] = _rms(x_ref[...], g_ref[...])


def _rmsnorm(x, g, tm):
    t, d = x.shape
    return pl.pallas_call(
        _rmsnorm_kernel, grid=(t // tm,),
        in_specs=[pl.BlockSpec((tm, d), lambda i: (i, 0)), _resident(g.shape)],
        out_specs=pl.BlockSpec((tm, d), lambda i: (i, 0)),
        out_shape=jax.ShapeDtypeStruct((t, d), F32),
        compiler_params=_cparams("parallel"), name="final_norm",
    )(x, g)


def _rot_cols(w):
    half = QK_ROPE // 2
    return jnp.concatenate([-w[..., half:], w[..., :half]], axis=-1)


def _prep_mla(w_dq, q_norm, w_uq, w_dkv, kv_norm, w_ukv, w_o):
    ql = w_dq.shape[1]
    kvl = w_ukv.shape[0]
    wq = w_uq.reshape(ql, N_HEADS, QK_NOPE + QK_ROPE)
    wq_nope, wq_rope = wq[..., :QK_NOPE], wq[..., QK_NOPE:]
    zpad = jnp.zeros((ql, N_HEADS, HEAD_PAD - QK_NOPE - QK_ROPE), F32)
    w1 = jnp.concatenate([wq_nope, wq_rope, zpad], axis=-1).reshape(ql, N_HEADS * HEAD_PAD)
    w2 = jnp.concatenate([jnp.zeros_like(wq_nope), _rot_cols(wq_rope), zpad], axis=-1).reshape(ql, N_HEADS * HEAD_PAD)
    kr_w = w_dkv[:, kvl:]
    lpad = jnp.zeros((w_dkv.shape[0], LANES - QK_ROPE), F32)
    wdkv = jnp.concatenate([w_dkv[:, :kvl], kr_w, lpad, _rot_cols(kr_w), lpad], axis=1)
    wkv = w_ukv.reshape(kvl, N_HEADS, QK_NOPE + V_DIM)
    wk_top = jnp.concatenate([wkv[..., :QK_NOPE], jnp.zeros((kvl, N_HEADS, HEAD_PAD - QK_NOPE), F32)], axis=-1)
    eye = jnp.eye(LANES, QK_ROPE, dtype=F32)
    wk_bot = jnp.concatenate([jnp.zeros((LANES, QK_NOPE), F32), eye,
                              jnp.zeros((LANES, HEAD_PAD - QK_NOPE - QK_ROPE), F32)], axis=-1)
    wk_bot = jnp.broadcast_to(wk_bot[:, None, :], (LANES, N_HEADS, HEAD_PAD))
    wk = jnp.concatenate([wk_top, wk_bot], axis=0).reshape(kvl + LANES, N_HEADS * HEAD_PAD)
    wvt = wkv[..., QK_NOPE:].reshape(kvl, N_HEADS * V_DIM).T
    return dict(wdq=w_dq.astype(BF16), qn=q_norm.reshape(1, -1), w1t=w1.T.astype(BF16), w2t=w2.T.astype(BF16),
                wdkv=wdkv.astype(BF16), kvn=kv_norm.reshape(1, -1), wk=wk.astype(BF16), wvt=wvt.astype(BF16),
                wo=w_o.astype(BF16))


def _rope_tables(s):
    pos = jnp.arange(s, dtype=F32)
    inv = ROPE_THETA ** (-jnp.arange(0, QK_ROPE, 2, dtype=F32) / QK_ROPE)
    ang = pos[:, None] * inv[None, :]
    ang = jnp.concatenate([ang, ang], axis=-1)
    cos, sin = jnp.cos(ang), jnp.sin(ang)
    qs = ATTN_SCALE * LOG2E
    zq = jnp.zeros((s, HEAD_PAD - QK_NOPE - QK_ROPE), F32)
    ctq = jnp.concatenate([jnp.full((s, QK_NOPE), qs, F32), cos * qs, zq], axis=1).T
    stq = jnp.concatenate([jnp.zeros((s, QK_NOPE), F32), sin * qs, zq], axis=1).T
    zk = jnp.zeros((s, LANES - QK_ROPE), F32)
    ctk = jnp.concatenate([cos, zk], axis=1)
    stk = jnp.concatenate([sin, zk], axis=1)
    return ctq, stq, ctk, stk


def _prep_s5(a_re, a_im, log_step, b_re, b_im, c_re, c_im):
    _, g, p = a_re.shape
    n = g * p * GROUP_CH
    rep = lambda a: jnp.broadcast_to(a[..., None], (2, g, p, GROUP_CH)).reshape(2, n)
    ls = jnp.broadcast_to(log_step[:, :, None, None], (2, g, p, GROUP_CH)).reshape(2, n)
    outs = pl.pallas_call(
        _s5_param_kernel,
        out_shape=[jax.ShapeDtypeStruct((2, n), F32)] * 4, name="s5_params",
    )(rep(a_re), rep(a_im), ls, b_re.reshape(2, n), b_im.reshape(2, n))
    lr, li, bbr, bbi = [o.reshape(2, g, p, GROUP_CH) for o in outs]
    lr, li = lr[..., 0], li[..., 0]
    gpt = MXU_DIM // GROUP_CH
    ntile = g // gpt
    eye = jnp.eye(gpt, dtype=F32)

    def tiles_b(bb):
        t = bb.reshape(2, ntile, gpt, p, GROUP_CH)
        return jnp.einsum('djapc,ab->djacbp', t, eye).reshape(2, ntile, gpt * GROUP_CH, gpt * p)

    def tiles_c(cc):
        t = cc.reshape(2, ntile, gpt, GROUP_CH, p)
        return jnp.einsum('djacp,ab->djapbc', t, eye).reshape(2, ntile, gpt * p, gpt * GROUP_CH)

    bt = jnp.concatenate([tiles_b(bbr), tiles_b(bbi)], axis=-1).astype(BF16)
    ct = jnp.concatenate([tiles_c(c_re), tiles_c(-c_im)], axis=-2).astype(BF16)
    lrt = lr.reshape(2, ntile, 1, gpt * p)
    lit = li.reshape(2, ntile, 1, gpt * p)
    return bt, ct, lrt, lit


def _split_bf16(w):
    hi = w.astype(BF16)
    lo = (w - hi.astype(F32)).astype(BF16)
    return jnp.stack([hi, lo])


def _cpow(lr, li, n):
    k = int(math.log2(n))
    assert 2 ** k == n
    for _ in range(k):
        lr, li = lr * lr - li * li, 2.0 * lr * li
    return lr, li


def _mla_layer(x, p, j, tabs, tm, tq, tk, unroll):
    w = p['mla'][j]
    qt, k, vt = _mla_pre(x, p['norm_mix'][2 * j], w, tabs, tm)
    ot = _attention(qt, k, vt, tq, tk, unroll)
    f = p['ffn'][j]
    return _attn_out_ffn(x, ot, w['wo'], p['norm_ffn'][2 * j], f['wg'], f['wu'], f['wd'], tm)


def _s5_states(xp, g, s5, nseg, tt):
    bt, ct, lr, li = s5
    nrg = xp.shape[0]
    ntile, half = bt.shape[1], bt.shape[3] // 2
    zero = jnp.zeros((nrg, ntile, 2, SUBLANES, half), F32)
    if nseg == 1:
        return zero, zero
    seg_len = xp.shape[1] // SUBLANES
    inits = []
    for k, reverse in ((0, False), (1, True)):
        _, fin = _s5_scan(xp, g, bt[k], ct[k], lr[k], li[k], zero, tt, reverse, False)
        e = fin.reshape(nrg, ntile, 2, SUBLANES // nseg, nseg, half)
        plr, pli = _cpow(lr[k], li[k], seg_len)
        plr, pli = plr[None], pli[None]
        order = range(nseg - 1, -1, -1) if reverse else range(nseg)
        sr = jnp.zeros_like(e[:, :, 0, :, 0])
        si = jnp.zeros_like(sr)
        init_r = [None] * nseg
        init_i = [None] * nseg
        for q in order:
            init_r[q], init_i[q] = sr, si
            er, ei = e[:, :, 0, :, q], e[:, :, 1, :, q]
            sr, si = plr * sr - pli * si + er, plr * si + pli * sr + ei
        init = jnp.stack([jnp.stack(init_r, axis=3), jnp.stack(init_i, axis=3)], axis=2)
        inits.append(init.reshape(nrg, ntile, 2, SUBLANES, half))
    return inits[0], inits[1]


def _s5_moe_layer(xp, p, j, nseg, tt, tm, last):
    nrg, rows, d = xp.shape
    i = 2 * j + 1
    s5 = p['s5'][j]
    bt, ct, lr, li = s5
    gm = p['norm_mix'][i]
    init_f, init_b = _s5_states(xp, gm, s5, nseg, tt)
    yf, _ = _s5_scan(xp, gm, bt[0], ct[0], lr[0], li[0], init_f, tt, False, True)
    yb, _ = _s5_scan(xp, gm, bt[1], ct[1], lr[1], li[1], init_b, tt, True, True)
    t = nrg * rows
    m = p['moe'][j]
    x1, h2, idx, gate = _s5_glu_router(xp.reshape(t, d), yf.reshape(t, d), yb.reshape(t, d), gm,
                                       p['s5_d'][j], p['s5_wglu'][j], p['norm_ffn'][i], m['wr'], tm)
    rank, cnt = _moe_plan(idx, tm)
    counts = cnt[0, :N_EXPERTS].astype(jnp.int32)
    ntiles = (counts + tm - 1) // tm
    tile_end = jnp.cumsum(ntiles)
    offs = (tile_end - ntiles) * tm
    e2 = idx[:, :2]
    pos = offs[e2] + rank[:, :2]
    nt_max = (2 * t) // tm + N_EXPERTS
    tile_expert = jnp.minimum(jnp.searchsorted(tile_end, jnp.arange(nt_max, dtype=jnp.int32), side='right'),
                              N_EXPERTS - 1).astype(jnp.int32)
    n_used = tile_end[-1:].astype(jnp.int32)
    xs = _moe_dispatch(h2, pos, nt_max * tm, tm)
    ys = _moe_experts(xs, tile_expert, n_used, m['wg'], m['wu'], m['wd'], tm)
    out = _moe_combine(x1, gate, ys, pos, p['norm_final'], tm, last)
    return out.reshape(nrg, rows, d)


def _to_perm(x, nseg):
    b, s, d = x.shape
    nrg = b * nseg // SUBLANES
    xr = x.reshape(nrg, SUBLANES, s // nseg, d)
    return jnp.swapaxes(xr, 1, 2).reshape(nrg, (s // nseg) * SUBLANES, d)


def _from_perm(xp, b, s, nseg):
    nrg, rows, d = xp.shape
    xr = xp.reshape(nrg, rows // SUBLANES, SUBLANES, d)
    return jnp.swapaxes(xr, 1, 2).reshape(b, s, d)


def _trunk(x, p):
    b, s, d = x.shape
    tm = _tile(s, 512)
    long_seq = s > 4096
    tq = _tile(s, 512 if long_seq else 2048, LANES)
    unroll = 4 if long_seq else 2
    tk = _tile(s, min(512, s // 4), LANES)
    nseg = 1 if b % SUBLANES == 0 else SUBLANES // b
    assert (b * nseg) % SUBLANES == 0 and s % nseg == 0
    tt = _tile(s // nseg, 64)
    tabs = _rope_tables(s)
    depth = p['norm_mix'].shape[0]
    for i in range(depth):
        j = i // 2
        if i % 2 == 0:
            x = _mla_layer(x, p, j, tabs, tm, tq, tk, unroll)
        else:
            xp = _s5_moe_layer(_to_perm(x, nseg), p, j, nseg, tt, tm, last=(i == depth - 1))
            x = _from_perm(xp, b, s, nseg)
    if depth % 2 == 1:
        x = _rmsnorm(x.reshape(b * s, d), p['norm_final'], tm).reshape(b, s, d)
    return x


def kernel(x_prompt, x_sample, norm_mix, norm_ffn, norm_final, mla_w_dq, mla_q_norm, mla_w_uq, mla_w_dkv,
           mla_kv_norm, mla_w_ukv, mla_w_o, ssm_a_re, ssm_a_im, ssm_log_step, ssm_b_re, ssm_b_im, ssm_c_re,
           ssm_c_im, ssm_d, ssm_w_glu, ffn_w_gate, ffn_w_up, ffn_w_down, moe_w_router, moe_w_gate, moe_w_up,
           moe_w_down):
    d = x_prompt.shape[-1]
    na, ns = mla_w_dq.shape[0], ssm_a_re.shape[0]
    p = dict(
        norm_mix=norm_mix.reshape(-1, 1, d), norm_ffn=norm_ffn.reshape(-1, 1, d), norm_final=norm_final.reshape(1, d),
        mla=[_prep_mla(mla_w_dq[j], mla_q_norm[j], mla_w_uq[j], mla_w_dkv[j], mla_kv_norm[j], mla_w_ukv[j],
                       mla_w_o[j]) for j in range(na)],
        ffn=[dict(wg=ffn_w_gate[j].astype(BF16), wu=ffn_w_up[j].astype(BF16), wd=ffn_w_down[j].astype(BF16))
             for j in range(na)],
        s5=[_prep_s5(ssm_a_re[j], ssm_a_im[j], ssm_log_step[j], ssm_b_re[j], ssm_b_im[j], ssm_c_re[j], ssm_c_im[j])
            for j in range(ns)],
        s5_d=[ssm_d[j].reshape(1, d) for j in range(ns)],
        s5_wglu=[ssm_w_glu[j].astype(BF16) for j in range(ns)],
        moe=[dict(wr=_split_bf16(jnp.pad(moe_w_router[j], ((0, 0), (0, LANES - N_EXPERTS)))),
                  wg=moe_w_gate[j].astype(BF16), wu=moe_w_up[j].astype(BF16), wd=moe_w_down[j].astype(BF16))
             for j in range(ns)],
    )
    return (_trunk(x_prompt, p), _trunk(x_sample, p))
```

```python
import functools
import math

import jax
import jax.numpy as jnp
from jax import lax
from jax.experimental import pallas as pl
from jax.experimental.pallas import tpu as pltpu

BF16 = jnp.bfloat16
F32 = jnp.float32

N_HEADS = 16
QK_NOPE = 64
QK_ROPE = 32
V_DIM = 64
HEAD_PAD = 128
ROPE_THETA = 10000.0
ATTN_SCALE = (QK_NOPE + QK_ROPE) ** -0.5
LOG2E = 1.4426950408889634
GROUP_CH = 16
STATE_DIM = 64
N_EXPERTS = 8
NORM_EPS = 1e-6
LANES = 128
SUBLANES = 8
MXU_DIM = 256
VMEM_LIMIT = 56 * 1024 * 1024
NEG_BIG = -1e30


def _cparams(*sem):
    return pltpu.CompilerParams(dimension_semantics=sem, vmem_limit_bytes=VMEM_LIMIT)


def _resident(shape):
    zeros = (0,) * len(shape)
    return pl.BlockSpec(shape, lambda *_: zeros, pipeline_mode=pl.Buffered(1))


def _tile(n, cap, mult=SUBLANES):
    if n <= cap:
        return n
    t = (cap // mult) * mult
    while t > mult and n % t:
        t -= mult
    assert n % t == 0, (n, cap, mult)
    return t


def _rms(x, g):
    ms = jnp.mean(x * x, axis=-1, keepdims=True)
    return x * lax.rsqrt(ms + NORM_EPS) * g


def _dot(a, b):
    return jnp.dot(a, b, preferred_element_type=F32)


def _dot_nt(a, b):
    return lax.dot_general(a, b, (((1,), (1,)), ((), ())), preferred_element_type=F32)


def _dot_tn(a, b):
    return lax.dot_general(a, b, (((0,), (0,)), ((), ())), preferred_element_type=F32)


def _swiglu(h, wg_ref, wu_ref, wd_ref, act_ref, fc):
    f = act_ref.shape[1]
    for c0 in range(0, f, fc):
        g = _dot(h, wg_ref[:, c0:c0 + fc])
        u = _dot(h, wu_ref[:, c0:c0 + fc])
        act_ref[:, c0:c0 + fc] = (g * jax.nn.sigmoid(g) * u).astype(BF16)
    return _dot(act_ref[...], wd_ref[...])


def _mla_pre_kernel(x_ref, g_ref, wdq_ref, qn_ref, w1_ref, w2_ref, wdkv_ref, kvn_ref, wk_ref, wv_ref,
                    ctq_ref, stq_ref, ctk_ref, stk_ref, qt_ref, k_ref, vt_ref):
    h = _rms(x_ref[0], g_ref[...]).astype(BF16)
    cq = _rms(_dot(h, wdq_ref[...]), qn_ref[...]).astype(BF16)
    a = _dot_nt(w1_ref[...], cq)
    b = _dot_nt(w2_ref[...], cq)
    ct = ctq_ref[...]
    st = stq_ref[...]
    for hh in range(N_HEADS):
        sl = slice(hh * HEAD_PAD, (hh + 1) * HEAD_PAD)
        qt_ref[0, sl, :] = (a[sl] * ct + b[sl] * st).astype(BF16)
    ck = _dot(h, wdkv_ref[...])
    kvl = kvn_ref.shape[1]
    ckv = _rms(ck[:, :kvl], kvn_ref[...]).astype(BF16)
    kr = (ck[:, kvl:kvl + LANES] * ctk_ref[...] + ck[:, kvl + LANES:] * stk_ref[...]).astype(BF16)
    kaug = jnp.concatenate([ckv, kr], axis=1)
    k_ref[0] = _dot(kaug, wk_ref[...]).astype(BF16)
    vt_ref[0] = _dot_nt(wv_ref[...], ckv).astype(BF16)


def _mla_pre(x, g, w, tabs, tm):
    b, s, d = x.shape
    hq = N_HEADS * HEAD_PAD
    hv = N_HEADS * V_DIM
    ctq, stq, ctk, stk = tabs
    grid = (b, s // tm)
    in_specs = [
        pl.BlockSpec((1, tm, d), lambda bi, i: (bi, i, 0)),
        _resident(g.shape), _resident(w['wdq'].shape), _resident(w['qn'].shape),
        _resident(w['w1t'].shape), _resident(w['w2t'].shape), _resident(w['wdkv'].shape),
        _resident(w['kvn'].shape), _resident(w['wk'].shape), _resident(w['wvt'].shape),
        pl.BlockSpec((HEAD_PAD, tm), lambda bi, i: (0, i)),
        pl.BlockSpec((HEAD_PAD, tm), lambda bi, i: (0, i)),
        pl.BlockSpec((tm, LANES), lambda bi, i: (i, 0)),
        pl.BlockSpec((tm, LANES), lambda bi, i: (i, 0)),
    ]
    out_specs = [
        pl.BlockSpec((1, hq, tm), lambda bi, i: (bi, 0, i)),
        pl.BlockSpec((1, tm, hq), lambda bi, i: (bi, i, 0)),
        pl.BlockSpec((1, hv, tm), lambda bi, i: (bi, 0, i)),
    ]
    out_shape = [jax.ShapeDtypeStruct((b, hq, s), BF16), jax.ShapeDtypeStruct((b, s, hq), BF16),
                 jax.ShapeDtypeStruct((b, hv, s), BF16)]
    return pl.pallas_call(
        _mla_pre_kernel, grid=grid, in_specs=in_specs, out_specs=out_specs, out_shape=out_shape,
        compiler_params=_cparams("parallel", "parallel"), name="mla_pre",
    )(x, g, w['wdq'], w['qn'], w['w1t'], w['w2t'], w['wdkv'], w['kvn'], w['wk'], w['wvt'], ctq, stq, ctk, stk)


SUM_ROWS = 16


def _attn_kernel(qt_ref, k_ref, vt_ref, o_ref, *rest, tk, unroll):
    s_refs, (acc_ref, m_ref) = rest[:-2], rest[-2:]
    nbuf = len(s_refs)
    qt = qt_ref[0]
    nk = k_ref.shape[1] // tk
    ones = jnp.ones((SUM_ROWS, tk), BF16)

    def scores(j, dst_ref):
        k0 = pl.multiple_of(j * tk, tk)
        dst_ref[...] = _dot(k_ref[0, pl.ds(k0, tk), :], qt)

    def update(j, src_ref):
        k0 = pl.multiple_of(j * tk, tk)
        s = src_ref[...]
        m_old = m_ref[...]
        m_new = jnp.maximum(m_old, jnp.max(s, axis=0, keepdims=True))
        p = jnp.exp2(s - m_new).astype(BF16)
        v = jnp.concatenate([vt_ref[0, :, pl.ds(k0, tk)], ones], axis=0)
        acc_ref[...] = jnp.exp2(m_old - m_new) * acc_ref[...] + _dot(v, p)
        m_ref[...] = m_new

    m_ref[...] = jnp.full(m_ref.shape, NEG_BIG, F32)
    acc_ref[...] = jnp.zeros(acc_ref.shape, F32)
    ahead = nbuf - 1
    for b in range(ahead):
        scores(b, s_refs[b])
    n_full = (nk - 1 - 2 * ahead) // nbuf + 1 if nk > 2 * ahead else 0

    def ring(t, c):
        j = t * nbuf
        for u in range(nbuf):
            scores(j + u + ahead, s_refs[(u + ahead) % nbuf])
            update(j + u, s_refs[u])
        return c

    lax.fori_loop(0, n_full, ring, 0, unroll=unroll)
    for b in range(n_full * nbuf, nk):
        if b + ahead < nk:
            scores(b + ahead, s_refs[(b + ahead) % nbuf])
        update(b, s_refs[b % nbuf])
    acc = acc_ref[...]
    o_ref[0] = (acc[:V_DIM] / acc[V_DIM:V_DIM + 1]).astype(BF16)


def _attention(qt, k, vt, tq, tk, unroll, nbuf):
    b, hq, s = qt.shape
    assert s // tk >= nbuf - 1
    grid = (b, N_HEADS, s // tq)
    return pl.pallas_call(
        functools.partial(_attn_kernel, tk=tk, unroll=unroll), grid=grid,
        in_specs=[pl.BlockSpec((1, HEAD_PAD, tq), lambda bi, h, i: (bi, h, i)),
                  pl.BlockSpec((1, s, HEAD_PAD), lambda bi, h, i: (bi, 0, h)),
                  pl.BlockSpec((1, V_DIM, s), lambda bi, h, i: (bi, h, 0))],
        out_specs=pl.BlockSpec((1, V_DIM, tq), lambda bi, h, i: (bi, h, i)),
        out_shape=jax.ShapeDtypeStruct((b, N_HEADS * V_DIM, s), BF16),
        scratch_shapes=[pltpu.VMEM((tk, tq), F32)] * nbuf + [
            pltpu.VMEM((V_DIM + SUM_ROWS, tq), F32), pltpu.VMEM((1, tq), F32)],
        compiler_params=_cparams("parallel", "parallel", "parallel"), name="mla_attention",
    )(qt, k, vt)


def _attn_out_ffn_kernel(x_ref, ot_ref, wo_ref, g_ref, wg_ref, wu_ref, wd_ref, y_ref, act_ref, *, fc):
    x1 = x_ref[0] + _dot_tn(ot_ref[0], wo_ref[...])
    h = _rms(x1, g_ref[...]).astype(BF16)
    y_ref[0] = x1 + _swiglu(h, wg_ref, wu_ref, wd_ref, act_ref, fc)


def _attn_out_ffn(x, ot, wo, g, wg, wu, wd, tm):
    b, s, d = x.shape
    f = wg.shape[1]
    fc = _tile(f, 1536, LANES)
    hv = ot.shape[1]
    return pl.pallas_call(
        functools.partial(_attn_out_ffn_kernel, fc=fc), grid=(b, s // tm),
        in_specs=[pl.BlockSpec((1, tm, d), lambda bi, i: (bi, i, 0)),
                  pl.BlockSpec((1, hv, tm), lambda bi, i: (bi, 0, i)),
                  _resident(wo.shape), _resident(g.shape), _resident(wg.shape), _resident(wu.shape),
                  _resident(wd.shape)],
        out_specs=pl.BlockSpec((1, tm, d), lambda bi, i: (bi, i, 0)),
        out_shape=jax.ShapeDtypeStruct((b, s, d), F32),
        scratch_shapes=[pltpu.VMEM((tm, f), BF16)],
        compiler_params=_cparams("parallel", "parallel"), name="attn_out_ffn",
    )(x, ot, wo, g, wg, wu, wd)


def _s5_param_kernel(are_ref, aim_ref, ls_ref, bre_ref, bim_ref, lr_ref, li_ref, bbr_ref, bbi_ref):
    ar = are_ref[...]
    ai = aim_ref[...]
    delta = jnp.exp(ls_ref[...])
    mag = jnp.exp(ar * delta)
    lr = mag * jnp.cos(ai * delta)
    li = mag * jnp.sin(ai * delta)
    nr = lr - 1.0
    den = ar * ar + ai * ai
    cr = (nr * ar + li * ai) / den
    ci = (li * ar - nr * ai) / den
    br = bre_ref[...]
    bi = bim_ref[...]
    lr_ref[...] = lr
    li_ref[...] = li
    bbr_ref[...] = cr * br - ci * bi
    bbi_ref[...] = cr * bi + ci * br


def _s5_scan_kernel(x_ref, g_ref, bt_ref, ct_ref, lr_ref, li_ref, init_ref, *rest, tt, reverse, emit_y):
    if emit_y:
        y_ref, fin_ref, bu0_ref, bu1_ref, st_ref = rest
    else:
        fin_ref, bu0_ref, bu1_ref, st_ref = rest
    c = pl.program_id(1)

    @pl.when(c == 0)
    def _():
        st_ref[...] = init_ref[0]

    hb = _rms(x_ref[0], g_ref[...]).astype(BF16)

    def write_y(j, y):
        y_ref[0, :, j * MXU_DIM:(j + 1) * MXU_DIM] = y

    _s5_direction(hb, bt_ref, ct_ref, lr_ref, li_ref, st_ref, (bu0_ref, bu1_ref), tt, reverse,
                  write_y if emit_y else None)

    @pl.when(c == pl.num_programs(1) - 1)
    def _():
        fin_ref[0] = st_ref[...]


def _s5_direction(hb, bt_ref, ct_ref, lr_ref, li_ref, st_ref, bu_refs, tt, reverse, write_y):
    for j in range(bt_ref.shape[0]):
        _s5_tile(hb, j, bt_ref, ct_ref, lr_ref, li_ref, st_ref, bu_refs[j % 2], tt, reverse, write_y)


def _s5_tile(hb, j, bt_ref, ct_ref, lr_ref, li_ref, st_ref, bu_ref, tt, reverse, write_y):
    half = bt_ref.shape[2] // 2
    bu_ref[...] = _dot(hb[:, j * MXU_DIM:(j + 1) * MXU_DIM], bt_ref[j])
    lr = jnp.broadcast_to(lr_ref[j], (SUBLANES, half))
    li = jnp.broadcast_to(li_ref[j], (SUBLANES, half))
    xr, xi = st_ref[j, 0], st_ref[j, 1]
    for t in range(tt):
        r0 = ((tt - 1 - t) if reverse else t) * SUBLANES
        nr = lr * xr - li * xi + bu_ref[r0:r0 + SUBLANES, 0:half]
        ni = lr * xi + li * xr + bu_ref[r0:r0 + SUBLANES, half:2 * half]
        bu_ref[r0:r0 + SUBLANES, 0:half] = nr
        bu_ref[r0:r0 + SUBLANES, half:2 * half] = ni
        xr, xi = nr, ni
    st_ref[j, 0] = xr
    st_ref[j, 1] = xi
    if write_y is not None:
        write_y(j, _dot(bu_ref[...].astype(BF16), ct_ref[j]))


def _s5_glu_router_kernel(x_ref, yf_ref, yb_ref, gm_ref, d_ref, wglu_ref, gf_ref, wr_ref,
                          x1_ref, h2_ref, idx_ref, gate_ref):
    tm = x_ref.shape[0]
    sub = tm // 2 if tm % (2 * SUBLANES) == 0 else tm
    for r0 in range(0, tm, sub):
        rows = slice(r0, r0 + sub)
        x1, h2, idx, gate = _glu_router_rows(x_ref[rows, :], yf_ref[rows, :] + yb_ref[rows, :], gm_ref, d_ref,
                                             wglu_ref, gf_ref, wr_ref)
        x1_ref[rows, :] = x1
        h2_ref[rows, :] = h2
        idx_ref[rows, :] = idx
        gate_ref[rows, :] = gate


def _glu_router_rows(x, ysum, gm_ref, d_ref, wglu_ref, gf_ref, wr_ref):
    d = x.shape[1]
    y = ysum + d_ref[...] * _rms(x, gm_ref[...])
    g = jax.nn.gelu(y, approximate=True).astype(BF16)
    z = _dot(g, wglu_ref[...])
    x1 = x + z[:, :d] * jax.nn.sigmoid(z[:, d:])
    h2 = _rms(x1, gf_ref[...])
    hi = h2.astype(BF16)
    lo = (h2 - hi.astype(F32)).astype(BF16)
    logits = _dot(hi, wr_ref[0]) + (_dot(lo, wr_ref[0]) + _dot(hi, wr_ref[1]))
    idx, gate = _topk2(logits)
    return x1, h2, idx, gate


def _s5_glu_router(x, yf, yb, gm, dskip, wglu, gf, wr, tm):
    t, d = x.shape
    row = pl.BlockSpec((tm, d), lambda i: (i, 0))
    nar = pl.BlockSpec((tm, LANES), lambda i: (i, 0))
    return pl.pallas_call(
        _s5_glu_router_kernel, grid=(t // tm,),
        in_specs=[row, row, row, _resident(gm.shape), _resident(dskip.shape), _resident(wglu.shape),
                  _resident(gf.shape), _resident(wr.shape)],
        out_specs=[row, row, nar, nar],
        out_shape=[jax.ShapeDtypeStruct((t, d), F32), jax.ShapeDtypeStruct((t, d), F32),
                   jax.ShapeDtypeStruct((t, LANES), jnp.int32), jax.ShapeDtypeStruct((t, LANES), F32)],
        compiler_params=_cparams("parallel"), name="s5_glu_router",
    )(x, yf, yb, gm, dskip, wglu, gf, wr)


def _s5_scan(xp, g, bt, ct, lr, li, init, tt, reverse, emit_y):
    nrg, rows, d = xp.shape
    s = rows // SUBLANES
    nc = s // tt
    ntile, _, sw = bt.shape
    half = sw // 2
    blk = tt * SUBLANES
    if reverse:
        xmap = lambda gi, c: (gi, nc - 1 - c, 0)
    else:
        xmap = lambda gi, c: (gi, c, 0)
    st_shape = (ntile, 2, SUBLANES, half)
    fin_spec = pl.BlockSpec((1,) + st_shape, lambda gi, c: (gi, 0, 0, 0, 0))
    fin_shape = jax.ShapeDtypeStruct((nrg,) + st_shape, F32)
    if emit_y:
        out_specs = [pl.BlockSpec((1, blk, d), xmap), fin_spec]
        out_shape = [jax.ShapeDtypeStruct(xp.shape, F32), fin_shape]
    else:
        out_specs = [fin_spec]
        out_shape = [fin_shape]
    outs = pl.pallas_call(
        functools.partial(_s5_scan_kernel, tt=tt, reverse=reverse, emit_y=emit_y), grid=(nrg, nc),
        in_specs=[pl.BlockSpec((1, blk, d), xmap), _resident(g.shape), _resident(bt.shape),
                  _resident(ct.shape), _resident(lr.shape), _resident(li.shape),
                  pl.BlockSpec((1,) + st_shape, lambda gi, c: (gi, 0, 0, 0, 0))],
        out_specs=out_specs, out_shape=out_shape,
        scratch_shapes=[pltpu.VMEM((blk, sw), F32), pltpu.VMEM((blk, sw), F32), pltpu.VMEM(st_shape, F32)],
        compiler_params=_cparams("parallel", "arbitrary"),
        name="s5_scan_" + ("bwd" if reverse else "fwd") + ("" if emit_y else "_state"),
    )(xp, g, bt, ct, lr, li, init)
    if emit_y:
        return outs[0], outs[1]
    return None, outs[0]


def _topk2(logits):
    lane = lax.broadcasted_iota(jnp.int32, logits.shape, 1)
    lg = jnp.where(lane < N_EXPERTS, logits, -jnp.inf)
    m1 = jnp.max(lg, axis=-1, keepdims=True)
    i1 = jnp.min(jnp.where(lg == m1, lane, LANES), axis=-1, keepdims=True)
    lg2 = jnp.where(lane == i1, -jnp.inf, lg)
    m2 = jnp.max(lg2, axis=-1, keepdims=True)
    i2 = jnp.min(jnp.where(lg2 == m2, lane, LANES), axis=-1, keepdims=True)
    e = jnp.exp(m2 - m1)
    g1 = 1.0 / (1.0 + e)
    g2 = e / (1.0 + e)
    idx = jnp.where(lane == 0, i1, jnp.where(lane == 1, i2, 0))
    gate = jnp.where(lane == 0, g1, jnp.where(lane == 1, g2, 0.0))
    return idx, gate


def _moe_plan_kernel(idx_ref, ltri_ref, rank_ref, cnt_ref, carry_ref):
    @pl.when(pl.program_id(0) == 0)
    def _():
        carry_ref[...] = jnp.zeros_like(carry_ref)

    idx = idx_ref[...]
    lane = lax.broadcasted_iota(jnp.int32, idx.shape, 1)
    oh1 = lane == idx[:, 0:1]
    oh2 = lane == idx[:, 1:2]
    oh = jnp.where(oh1, 1.0, jnp.where(oh2, 1.0, 0.0)).astype(BF16)
    cs = _dot(ltri_ref[...], oh)
    tot = carry_ref[...] + cs - 1.0
    r1 = jnp.sum(jnp.where(oh1, tot, 0.0), axis=-1, keepdims=True)
    r2 = jnp.sum(jnp.where(oh2, tot, 0.0), axis=-1, keepdims=True)
    rank_ref[...] = jnp.where(lane == 0, r1, jnp.where(lane == 1, r2, 0.0)).astype(jnp.int32)
    new = carry_ref[...] + cs[cs.shape[0] - 1:, :]
    carry_ref[...] = new
    cnt_ref[...] = new


def _moe_plan(idx, tm):
    t = idx.shape[0]
    ltri = jnp.tril(jnp.ones((tm, tm), F32)).astype(BF16)
    nar = pl.BlockSpec((tm, LANES), lambda i: (i, 0))
    return pl.pallas_call(
        _moe_plan_kernel, grid=(t // tm,),
        in_specs=[nar, _resident(ltri.shape)],
        out_specs=[nar, pl.BlockSpec((1, LANES), lambda i: (0, 0))],
        out_shape=[jax.ShapeDtypeStruct((t, LANES), jnp.int32), jax.ShapeDtypeStruct((1, LANES), F32)],
        scratch_shapes=[pltpu.VMEM((1, LANES), F32)],
        compiler_params=_cparams("arbitrary"), name="moe_plan",
    )(idx, ltri)


ROW_UNROLL = 8


def _row_copy(src, dst, i, j, sem):
    return pltpu.make_async_copy(src.at[pl.ds(i, 1)], dst.at[pl.ds(j, 1)], sem)


def _moe_dispatch_kernel(pos_ref, x_ref, xs_in, xs_hbm, sem, *, tm):
    del xs_in

    def issue(g, c):
        for u in range(ROW_UNROLL):
            r = g * ROW_UNROLL + u
            _row_copy(x_ref, xs_hbm, r, pos_ref[0, 0, 2 * r], sem).start(priority=0)
            _row_copy(x_ref, xs_hbm, r, pos_ref[0, 0, 2 * r + 1], sem).start(priority=1)
        return c

    lax.fori_loop(0, tm // ROW_UNROLL, issue, 0)

    def drain(g, c):
        for _ in range(2 * ROW_UNROLL):
            _row_copy(x_ref, xs_hbm, 0, 0, sem).wait()
        return c

    lax.fori_loop(0, tm // ROW_UNROLL, drain, 0)


def _moe_dispatch(h2, pos, rows, tm):
    t, d = h2.shape
    pos3 = pos.reshape(t // tm, 1, 2 * tm)
    xs0 = jnp.zeros((rows, d), F32)
    return pl.pallas_call(
        functools.partial(_moe_dispatch_kernel, tm=tm), grid=(t // tm,),
        in_specs=[pl.BlockSpec((1, 1, 2 * tm), lambda i: (i, 0, 0), memory_space=pltpu.SMEM),
                  pl.BlockSpec((tm, d), lambda i: (i, 0)), pl.BlockSpec(memory_space=pl.ANY)],
        out_specs=pl.BlockSpec(memory_space=pl.ANY),
        out_shape=jax.ShapeDtypeStruct((rows, d), F32),
        scratch_shapes=[pltpu.SemaphoreType.DMA(())],
        input_output_aliases={2: 0},
        compiler_params=_cparams("arbitrary"), name="moe_dispatch",
    )(pos3, h2, xs0)


def _moe_expert_kernel(te_ref, nu_ref, xs_ref, wg_ref, wu_ref, wd_ref, y_ref, act_ref, *, fc):
    i = pl.program_id(0)

    @pl.when(i < nu_ref[0])
    def _():
        y_ref[...] = _swiglu(xs_ref[...].astype(BF16), wg_ref.at[0], wu_ref.at[0], wd_ref.at[0], act_ref, fc)

    @pl.when(i >= nu_ref[0])
    def _():
        y_ref[...] = jnp.zeros_like(y_ref)


def _moe_experts(xs, tile_expert, n_used, wg, wu, wd, tm):
    rows, d = xs.shape
    f = wg.shape[2]
    fc = _tile(f, 1792, LANES)
    wspec = lambda shp: pl.BlockSpec((1,) + shp, lambda i, te, nu: (te[i], 0, 0), pipeline_mode=pl.Buffered(1))
    gs = pltpu.PrefetchScalarGridSpec(
        num_scalar_prefetch=2, grid=(rows // tm,),
        in_specs=[pl.BlockSpec((tm, d), lambda i, te, nu: (i, 0)),
                  wspec((d, f)), wspec((d, f)), wspec((f, d))],
        out_specs=pl.BlockSpec((tm, d), lambda i, te, nu: (i, 0)),
        scratch_shapes=[pltpu.VMEM((tm, f), BF16)])
    return pl.pallas_call(
        functools.partial(_moe_expert_kernel, fc=fc), grid_spec=gs,
        out_shape=jax.ShapeDtypeStruct((rows, d), F32),
        compiler_params=_cparams("arbitrary"), name="moe_experts",
    )(tile_expert, n_used, xs, wg, wu, wd)


def _moe_combine_kernel(pos_ref, x_ref, gate_ref, ys_hbm, gf_ref, o_ref, b0_ref, b1_ref, sem, *, tm, final_norm):
    def issue(g, c):
        for u in range(ROW_UNROLL):
            r = g * ROW_UNROLL + u
            _row_copy(ys_hbm, b0_ref, pos_ref[0, 0, 2 * r], r, sem).start(priority=0)
            _row_copy(ys_hbm, b1_ref, pos_ref[0, 0, 2 * r + 1], r, sem).start(priority=1)
        return c

    lax.fori_loop(0, tm // ROW_UNROLL, issue, 0)

    def drain(g, c):
        for _ in range(ROW_UNROLL):
            _row_copy(ys_hbm, b0_ref, 0, 0, sem).wait()
            _row_copy(ys_hbm, b1_ref, 0, 0, sem).wait()
        return c

    lax.fori_loop(0, tm // ROW_UNROLL, drain, 0)
    gate = gate_ref[...]
    out = x_ref[...] + gate[:, 0:1] * b0_ref[...] + gate[:, 1:2] * b1_ref[...]
    if final_norm:
        out = _rms(out, gf_ref[...])
    o_ref[...] = out


def _moe_combine(x1, gate, ys, pos, gfin, tm, final_norm):
    t, d = x1.shape
    pos3 = pos.reshape(t // tm, 1, 2 * tm)
    return pl.pallas_call(
        functools.partial(_moe_combine_kernel, tm=tm, final_norm=final_norm), grid=(t // tm,),
        in_specs=[pl.BlockSpec((1, 1, 2 * tm), lambda i: (i, 0, 0), memory_space=pltpu.SMEM),
                  pl.BlockSpec((tm, d), lambda i: (i, 0)),
                  pl.BlockSpec((tm, LANES), lambda i: (i, 0)),
                  pl.BlockSpec(memory_space=pl.ANY), _resident(gfin.shape)],
        out_specs=pl.BlockSpec((tm, d), lambda i: (i, 0)),
        out_shape=jax.ShapeDtypeStruct((t, d), F32),
        scratch_shapes=[pltpu.VMEM((tm, d), F32), pltpu.VMEM((tm, d), F32), pltpu.SemaphoreType.DMA(())],
        compiler_params=_cparams("arbitrary"), name="moe_combine",
    )(pos3, x1, gate, ys, gfin)


def _rmsnorm_kernel(x_ref, g_ref, o_ref):
    o_ref[...] = _rms(x_ref[...], g_ref[...])


def _rmsnorm(x, g, tm):
    t, d = x.shape
    return pl.pallas_call(
        _rmsnorm_kernel, grid=(t // tm,),
        in_specs=[pl.BlockSpec((tm, d), lambda i: (i, 0)), _resident(g.shape)],
        out_specs=pl.BlockSpec((tm, d), lambda i: (i, 0)),
        out_shape=jax.ShapeDtypeStruct((t, d), F32),
        compiler_params=_cparams("parallel"), name="final_norm",
    )(x, g)


def _rot_cols(w):
    half = QK_ROPE // 2
    return jnp.concatenate([-w[..., half:], w[..., :half]], axis=-1)


def _prep_mla(w_dq, q_norm, w_uq, w_dkv, kv_norm, w_ukv, w_o):
    ql = w_dq.shape[1]
    kvl = w_ukv.shape[0]
    wq = w_uq.reshape(ql, N_HEADS, QK_NOPE + QK_ROPE)
    wq_nope, wq_rope = wq[..., :QK_NOPE], wq[..., QK_NOPE:]
    zpad = jnp.zeros((ql, N_HEADS, HEAD_PAD - QK_NOPE - QK_ROPE), F32)
    w1 = jnp.concatenate([wq_nope, wq_rope, zpad], axis=-1).reshape(ql, N_HEADS * HEAD_PAD)
    w2 = jnp.concatenate([jnp.zeros_like(wq_nope), _rot_cols(wq_rope), zpad], axis=-1).reshape(ql, N_HEADS * HEAD_PAD)
    kr_w = w_dkv[:, kvl:]
    lpad = jnp.zeros((w_dkv.shape[0], LANES - QK_ROPE), F32)
    wdkv = jnp.concatenate([w_dkv[:, :kvl], kr_w, lpad, _rot_cols(kr_w), lpad], axis=1)
    wkv = w_ukv.reshape(kvl, N_HEADS, QK_NOPE + V_DIM)
    wk_top = jnp.concatenate([wkv[..., :QK_NOPE], jnp.zeros((kvl, N_HEADS, HEAD_PAD - QK_NOPE), F32)], axis=-1)
    eye = jnp.eye(LANES, QK_ROPE, dtype=F32)
    wk_bot = jnp.concatenate([jnp.zeros((LANES, QK_NOPE), F32), eye,
                              jnp.zeros((LANES, HEAD_PAD - QK_NOPE - QK_ROPE), F32)], axis=-1)
    wk_bot = jnp.broadcast_to(wk_bot[:, None, :], (LANES, N_HEADS, HEAD_PAD))
    wk = jnp.concatenate([wk_top, wk_bot], axis=0).reshape(kvl + LANES, N_HEADS * HEAD_PAD)
    wvt = wkv[..., QK_NOPE:].reshape(kvl, N_HEADS * V_DIM).T
    return dict(wdq=w_dq.astype(BF16), qn=q_norm.reshape(1, -1), w1t=w1.T.astype(BF16), w2t=w2.T.astype(BF16),
                wdkv=wdkv.astype(BF16), kvn=kv_norm.reshape(1, -1), wk=wk.astype(BF16), wvt=wvt.astype(BF16),
                wo=w_o.astype(BF16))


def _rope_tables(s):
    pos = jnp.arange(s, dtype=F32)
    inv = ROPE_THETA ** (-jnp.arange(0, QK_ROPE, 2, dtype=F32) / QK_ROPE)
    ang = pos[:, None] * inv[None, :]
    ang = jnp.concatenate([ang, ang], axis=-1)
    cos, sin = jnp.cos(ang), jnp.sin(ang)
    qs = ATTN_SCALE * LOG2E
    zq = jnp.zeros((s, HEAD_PAD - QK_NOPE - QK_ROPE), F32)
    ctq = jnp.concatenate([jnp.full((s, QK_NOPE), qs, F32), cos * qs, zq], axis=1).T
    stq = jnp.concatenate([jnp.zeros((s, QK_NOPE), F32), sin * qs, zq], axis=1).T
    zk = jnp.zeros((s, LANES - QK_ROPE), F32)
    ctk = jnp.concatenate([cos, zk], axis=1)
    stk = jnp.concatenate([sin, zk], axis=1)
    return ctq, stq, ctk, stk


def _prep_s5(a_re, a_im, log_step, b_re, b_im, c_re, c_im):
    _, g, p = a_re.shape
    n = g * p * GROUP_CH
    rep = lambda a: jnp.broadcast_to(a[..., None], (2, g, p, GROUP_CH)).reshape(2, n)
    ls = jnp.broadcast_to(log_step[:, :, None, None], (2, g, p, GROUP_CH)).reshape(2, n)
    outs = pl.pallas_call(
        _s5_param_kernel,
        out_shape=[jax.ShapeDtypeStruct((2, n), F32)] * 4, name="s5_params",
    )(rep(a_re), rep(a_im), ls, b_re.reshape(2, n), b_im.reshape(2, n))
    lr, li, bbr, bbi = [o.reshape(2, g, p, GROUP_CH) for o in outs]
    lr, li = lr[..., 0], li[..., 0]
    gpt = MXU_DIM // GROUP_CH
    ntile = g // gpt
    eye = jnp.eye(gpt, dtype=F32)

    def tiles_b(bb):
        t = bb.reshape(2, ntile, gpt, p, GROUP_CH)
        return jnp.einsum('djapc,ab->djacbp', t, eye).reshape(2, ntile, gpt * GROUP_CH, gpt * p)

    def tiles_c(cc):
        t = cc.reshape(2, ntile, gpt, GROUP_CH, p)
        return jnp.einsum('djacp,ab->djapbc', t, eye).reshape(2, ntile, gpt * p, gpt * GROUP_CH)

    bt = jnp.concatenate([tiles_b(bbr), tiles_b(bbi)], axis=-1).astype(BF16)
    ct = jnp.concatenate([tiles_c(c_re), tiles_c(-c_im)], axis=-2).astype(BF16)
    lrt = lr.reshape(2, ntile, 1, gpt * p)
    lit = li.reshape(2, ntile, 1, gpt * p)
    return bt, ct, lrt, lit


def _split_bf16(w):
    hi = w.astype(BF16)
    lo = (w - hi.astype(F32)).astype(BF16)
    return jnp.stack([hi, lo])


def _cpow(lr, li, n):
    k = int(math.log2(n))
    assert 2 ** k == n
    for _ in range(k):
        lr, li = lr * lr - li * li, 2.0 * lr * li
    return lr, li


def _mla_layer(x, p, j, tabs, tm, attn_cfg):
    w = p['mla'][j]
    qt, k, vt = _mla_pre(x, p['norm_mix'][2 * j], w, tabs, tm)
    ot = _attention(qt, k, vt, *attn_cfg)
    f = p['ffn'][j]
    return _attn_out_ffn(x, ot, w['wo'], p['norm_ffn'][2 * j], f['wg'], f['wu'], f['wd'], tm)


def _s5_states(xp, g, s5, nseg, tt):
    bt, ct, lr, li = s5
    nrg = xp.shape[0]
    ntile, half = bt.shape[1], bt.shape[3] // 2
    zero = jnp.zeros((nrg, ntile, 2, SUBLANES, half), F32)
    if nseg == 1:
        return zero, zero
    seg_len = xp.shape[1] // SUBLANES
    inits = []
    for k, reverse in ((0, False), (1, True)):
        _, fin = _s5_scan(xp, g, bt[k], ct[k], lr[k], li[k], zero, tt, reverse, False)
        e = fin.reshape(nrg, ntile, 2, SUBLANES // nseg, nseg, half)
        plr, pli = _cpow(lr[k], li[k], seg_len)
        plr, pli = plr[None], pli[None]
        order = range(nseg - 1, -1, -1) if reverse else range(nseg)
        sr = jnp.zeros_like(e[:, :, 0, :, 0])
        si = jnp.zeros_like(sr)
        init_r = [None] * nseg
        init_i = [None] * nseg
        for q in order:
            init_r[q], init_i[q] = sr, si
            er, ei = e[:, :, 0, :, q], e[:, :, 1, :, q]
            sr, si = plr * sr - pli * si + er, plr * si + pli * sr + ei
        init = jnp.stack([jnp.stack(init_r, axis=3), jnp.stack(init_i, axis=3)], axis=2)
        inits.append(init.reshape(nrg, ntile, 2, SUBLANES, half))
    return inits[0], inits[1]


def _s5_moe_layer(xp, p, j, nseg, tt, tm, last):
    nrg, rows, d = xp.shape
    i = 2 * j + 1
    s5 = p['s5'][j]
    bt, ct, lr, li = s5
    gm = p['norm_mix'][i]
    init_f, init_b = _s5_states(xp, gm, s5, nseg, tt)
    yf, _ = _s5_scan(xp, gm, bt[0], ct[0], lr[0], li[0], init_f, tt, False, True)
    yb, _ = _s5_scan(xp, gm, bt[1], ct[1], lr[1], li[1], init_b, tt, True, True)
    t = nrg * rows
    m = p['moe'][j]
    x1, h2, idx, gate = _s5_glu_router(xp.reshape(t, d), yf.reshape(t, d), yb.reshape(t, d), gm,
                                       p['s5_d'][j], p['s5_wglu'][j], p['norm_ffn'][i], m['wr'], tm)
    rank, cnt = _moe_plan(idx, tm)
    counts = cnt[0, :N_EXPERTS].astype(jnp.int32)
    ntiles = (counts + tm - 1) // tm
    tile_end = jnp.cumsum(ntiles)
    offs = (tile_end - ntiles) * tm
    e2 = idx[:, :2]
    pos = offs[e2] + rank[:, :2]
    nt_max = (2 * t) // tm + N_EXPERTS
    tile_expert = jnp.minimum(jnp.searchsorted(tile_end, jnp.arange(nt_max, dtype=jnp.int32), side='right'),
                              N_EXPERTS - 1).astype(jnp.int32)
    n_used = tile_end[-1:].astype(jnp.int32)
    xs = _moe_dispatch(h2, pos, nt_max * tm, tm)
    ys = _moe_experts(xs, tile_expert, n_used, m['wg'], m['wu'], m['wd'], tm)
    out = _moe_combine(x1, gate, ys, pos, p['norm_final'], tm, last)
    return out.reshape(nrg, rows, d)


def _to_perm(x, nseg):
    b, s, d = x.shape
    nrg = b * nseg // SUBLANES
    xr = x.reshape(nrg, SUBLANES, s // nseg, d)
    return jnp.swapaxes(xr, 1, 2).reshape(nrg, (s // nseg) * SUBLANES, d)


def _from_perm(xp, b, s, nseg):
    nrg, rows, d = xp.shape
    xr = xp.reshape(nrg, rows // SUBLANES, SUBLANES, d)
    return jnp.swapaxes(xr, 1, 2).reshape(b, s, d)


def _trunk(x, p):
    b, s, d = x.shape
    tm = _tile(s, 512)
    long_seq = s > 4096
    tq = _tile(s, 512 if long_seq else 2048, LANES)
    tk = _tile(s, min(512, s // 4), LANES)
    attn_cfg = (tq, tk, 4 if long_seq else 2, 2)
    nseg = 1 if b % SUBLANES == 0 else SUBLANES // b
    assert (b * nseg) % SUBLANES == 0 and s % nseg == 0
    tt = _tile(s // nseg, 64)
    tabs = _rope_tables(s)
    depth = p['norm_mix'].shape[0]
    for i in range(depth):
        j = i // 2
        if i % 2 == 0:
            x = _mla_layer(x, p, j, tabs, tm, attn_cfg)
        else:
            xp = _s5_moe_layer(_to_perm(x, nseg), p, j, nseg, tt, tm, last=(i == depth - 1))
            x = _from_perm(xp, b, s, nseg)
    if depth % 2 == 1:
        x = _rmsnorm(x.reshape(b * s, d), p['norm_final'], tm).reshape(b, s, d)
    return x


def kernel(x_prompt, x_sample, norm_mix, norm_ffn, norm_final, mla_w_dq, mla_q_norm, mla_w_uq, mla_w_dkv,
           mla_kv_norm, mla_w_ukv, mla_w_o, ssm_a_re, ssm_a_im, ssm_log_step, ssm_b_re, ssm_b_im, ssm_c_re,
           ssm_c_im, ssm_d, ssm_w_glu, ffn_w_gate, ffn_w_up, ffn_w_down, moe_w_router, moe_w_gate, moe_w_up,
           moe_w_down):
    d = x_prompt.shape[-1]
    na, ns = mla_w_dq.shape[0], ssm_a_re.shape[0]
    p = dict(
        norm_mix=norm_mix.reshape(-1, 1, d), norm_ffn=norm_ffn.reshape(-1, 1, d), norm_final=norm_final.reshape(1, d),
        mla=[_prep_mla(mla_w_dq[j], mla_q_norm[j], mla_w_uq[j], mla_w_dkv[j], mla_kv_norm[j], mla_w_ukv[j],
                       mla_w_o[j]) for j in range(na)],
        ffn=[dict(wg=ffn_w_gate[j].astype(BF16), wu=ffn_w_up[j].astype(BF16), wd=ffn_w_down[j].astype(BF16))
             for j in range(na)],
        s5=[_prep_s5(ssm_a_re[j], ssm_a_im[j], ssm_log_step[j], ssm_b_re[j], ssm_b_im[j], ssm_c_re[j], ssm_c_im[j])
            for j in range(ns)],
        s5_d=[ssm_d[j].reshape(1, d) for j in range(ns)],
        s5_wglu=[ssm_w_glu[j].astype(BF16) for j in range(ns)],
        moe=[dict(wr=_split_bf16(jnp.pad(moe_w_router[j], ((0, 0), (0, LANES - N_EXPERTS)))),
                  wg=moe_w_gate[j].astype(BF16), wu=moe_w_up[j].astype(BF16), wd=moe_w_down[j].astype(BF16))
             for j in range(ns)],
    )
    return (_trunk(x_prompt, p), _trunk(x_sample, p))
```

```python
import functools
import math

import jax
import jax.numpy as jnp
from jax import lax
from jax.experimental import pallas as pl
from jax.experimental.pallas import tpu as pltpu

BF16 = jnp.bfloat16
F32 = jnp.float32

N_HEADS = 16
QK_NOPE = 64
QK_ROPE = 32
V_DIM = 64
HEAD_PAD = 128
ROPE_THETA = 10000.0
ATTN_SCALE = (QK_NOPE + QK_ROPE) ** -0.5
LOG2E = 1.4426950408889634
GROUP_CH = 16
STATE_DIM = 64
N_EXPERTS = 8
NORM_EPS = 1e-6
LANES = 128
SUBLANES = 8
MXU_DIM = 256
VMEM_LIMIT = 56 * 1024 * 1024
NEG_BIG = -1e30


def _cparams(*sem):
    return pltpu.CompilerParams(dimension_semantics=sem, vmem_limit_bytes=VMEM_LIMIT)


def _resident(shape):
    zeros = (0,) * len(shape)
    return pl.BlockSpec(shape, lambda *_: zeros, pipeline_mode=pl.Buffered(1))


def _tile(n, cap, mult=SUBLANES):
    if n <= cap:
        return n
    t = (cap // mult) * mult
    while t > mult and n % t:
        t -= mult
    assert n % t == 0, (n, cap, mult)
    return t


def _rms(x, g):
    ms = jnp.mean(x * x, axis=-1, keepdims=True)
    return x * lax.rsqrt(ms + NORM_EPS) * g


def _dot(a, b):
    return jnp.dot(a, b, preferred_element_type=F32)


def _dot_nt(a, b):
    return lax.dot_general(a, b, (((1,), (1,)), ((), ())), preferred_element_type=F32)


def _dot_tn(a, b):
    return lax.dot_general(a, b, (((0,), (0,)), ((), ())), preferred_element_type=F32)


def _swiglu(h, wg_ref, wu_ref, wd_ref, act_ref, fc):
    f = act_ref.shape[1]
    for c0 in range(0, f, fc):
        g = _dot(h, wg_ref[:, c0:c0 + fc])
        u = _dot(h, wu_ref[:, c0:c0 + fc])
        act_ref[:, c0:c0 + fc] = (g * jax.nn.sigmoid(g) * u).astype(BF16)
    return _dot(act_ref[...], wd_ref[...])


def _mla_pre_kernel(x_ref, g_ref, wdq_ref, qn_ref, w1_ref, w2_ref, wdkv_ref, kvn_ref, wk_ref, wv_ref,
                    ctq_ref, stq_ref, ctk_ref, stk_ref, qt_ref, k_ref, vt_ref):
    h = _rms(x_ref[0], g_ref[...]).astype(BF16)
    cq = _rms(_dot(h, wdq_ref[...]), qn_ref[...]).astype(BF16)
    a = _dot_nt(w1_ref[...], cq)
    b = _dot_nt(w2_ref[...], cq)
    ct = ctq_ref[...]
    st = stq_ref[...]
    r0, r1 = QK_NOPE, QK_NOPE + QK_ROPE
    for hh in range(N_HEADS):
        ah = a[hh * HEAD_PAD:(hh + 1) * HEAD_PAD] * ct
        rope = ah[r0:r1] + b[hh * QK_ROPE:(hh + 1) * QK_ROPE] * st
        qt_ref[0, hh * HEAD_PAD:(hh + 1) * HEAD_PAD, :] = jnp.concatenate([ah[:r0], rope, ah[r1:]], axis=0).astype(BF16)
    ck = _dot(h, wdkv_ref[...])
    kvl = kvn_ref.shape[1]
    ckv = _rms(ck[:, :kvl], kvn_ref[...]).astype(BF16)
    kr = (ck[:, kvl:kvl + LANES] * ctk_ref[...] + ck[:, kvl + LANES:] * stk_ref[...]).astype(BF16)
    kaug = jnp.concatenate([ckv, kr], axis=1)
    k_ref[0] = _dot(kaug, wk_ref[...]).astype(BF16)
    vt_ref[0] = _dot_nt(wv_ref[...], ckv).astype(BF16)


def _mla_pre(x, g, w, tabs, tm):
    b, s, d = x.shape
    hq = N_HEADS * HEAD_PAD
    hv = N_HEADS * V_DIM
    ctq, stq, ctk, stk = tabs
    grid = (b, s // tm)
    in_specs = [
        pl.BlockSpec((1, tm, d), lambda bi, i: (bi, i, 0)),
        _resident(g.shape), _resident(w['wdq'].shape), _resident(w['qn'].shape),
        _resident(w['w1t'].shape), _resident(w['w2t'].shape), _resident(w['wdkv'].shape),
        _resident(w['kvn'].shape), _resident(w['wk'].shape), _resident(w['wvt'].shape),
        pl.BlockSpec((HEAD_PAD, tm), lambda bi, i: (0, i)),
        pl.BlockSpec((QK_ROPE, tm), lambda bi, i: (0, i)),
        pl.BlockSpec((tm, LANES), lambda bi, i: (i, 0)),
        pl.BlockSpec((tm, LANES), lambda bi, i: (i, 0)),
    ]
    out_specs = [
        pl.BlockSpec((1, hq, tm), lambda bi, i: (bi, 0, i)),
        pl.BlockSpec((1, tm, hq), lambda bi, i: (bi, i, 0)),
        pl.BlockSpec((1, hv, tm), lambda bi, i: (bi, 0, i)),
    ]
    out_shape = [jax.ShapeDtypeStruct((b, hq, s), BF16), jax.ShapeDtypeStruct((b, s, hq), BF16),
                 jax.ShapeDtypeStruct((b, hv, s), BF16)]
    return pl.pallas_call(
        _mla_pre_kernel, grid=grid, in_specs=in_specs, out_specs=out_specs, out_shape=out_shape,
        compiler_params=_cparams("parallel", "parallel"), name="mla_pre",
    )(x, g, w['wdq'], w['qn'], w['w1t'], w['w2t'], w['wdkv'], w['kvn'], w['wk'], w['wvt'], ctq, stq, ctk, stk)


SUM_ROWS = 16


def _attn_kernel(qt_ref, k_ref, vt_ref, o_ref, *rest, tk, unroll):
    s_refs, (acc_ref, m_ref) = rest[:-2], rest[-2:]
    nbuf = len(s_refs)
    qt = qt_ref[0]
    nk = k_ref.shape[1] // tk
    ones = jnp.ones((SUM_ROWS, tk), BF16)

    def scores(j, dst_ref):
        k0 = pl.multiple_of(j * tk, tk)
        dst_ref[...] = _dot(k_ref[0, pl.ds(k0, tk), :], qt)

    def update(j, src_ref):
        k0 = pl.multiple_of(j * tk, tk)
        s = src_ref[...]
        m_old = m_ref[...]
        m_new = jnp.maximum(m_old, jnp.max(s, axis=0, keepdims=True))
        p = jnp.exp2(s - m_new).astype(BF16)
        v = jnp.concatenate([vt_ref[0, :, pl.ds(k0, tk)], ones], axis=0)
        acc_ref[...] = jnp.exp2(m_old - m_new) * acc_ref[...] + _dot(v, p)
        m_ref[...] = m_new

    m_ref[...] = jnp.full(m_ref.shape, NEG_BIG, F32)
    acc_ref[...] = jnp.zeros(acc_ref.shape, F32)
    ahead = nbuf - 1
    for b in range(ahead):
        scores(b, s_refs[b])
    n_full = (nk - 1 - 2 * ahead) // nbuf + 1 if nk > 2 * ahead else 0

    def ring(t, c):
        j = t * nbuf
        for u in range(nbuf):
            scores(j + u + ahead, s_refs[(u + ahead) % nbuf])
            update(j + u, s_refs[u])
        return c

    lax.fori_loop(0, n_full, ring, 0, unroll=unroll)
    for b in range(n_full * nbuf, nk):
        if b + ahead < nk:
            scores(b + ahead, s_refs[(b + ahead) % nbuf])
        update(b, s_refs[b % nbuf])
    acc = acc_ref[...]
    o_ref[0] = (acc[:V_DIM] / acc[V_DIM:V_DIM + 1]).astype(BF16)


def _attention(qt, k, vt, tq, tk, unroll, nbuf):
    b, hq, s = qt.shape
    assert s // tk >= nbuf - 1
    grid = (b, N_HEADS, s // tq)
    return pl.pallas_call(
        functools.partial(_attn_kernel, tk=tk, unroll=unroll), grid=grid,
        in_specs=[pl.BlockSpec((1, HEAD_PAD, tq), lambda bi, h, i: (bi, h, i)),
                  pl.BlockSpec((1, s, HEAD_PAD), lambda bi, h, i: (bi, 0, h)),
                  pl.BlockSpec((1, V_DIM, s), lambda bi, h, i: (bi, h, 0))],
        out_specs=pl.BlockSpec((1, V_DIM, tq), lambda bi, h, i: (bi, h, i)),
        out_shape=jax.ShapeDtypeStruct((b, N_HEADS * V_DIM, s), BF16),
        scratch_shapes=[pltpu.VMEM((tk, tq), F32)] * nbuf + [
            pltpu.VMEM((V_DIM + SUM_ROWS, tq), F32), pltpu.VMEM((1, tq), F32)],
        compiler_params=_cparams("parallel", "parallel", "parallel"), name="mla_attention",
    )(qt, k, vt)


def _attn_out_ffn_kernel(x_ref, ot_ref, wo_ref, g_ref, wg_ref, wu_ref, wd_ref, y_ref, act_ref, *, fc):
    x1 = x_ref[0] + _dot_tn(ot_ref[0], wo_ref[...])
    h = _rms(x1, g_ref[...]).astype(BF16)
    y_ref[0] = x1 + _swiglu(h, wg_ref, wu_ref, wd_ref, act_ref, fc)


def _attn_out_ffn(x, ot, wo, g, wg, wu, wd, tm):
    b, s, d = x.shape
    f = wg.shape[1]
    fc = _tile(f, 1536, LANES)
    hv = ot.shape[1]
    return pl.pallas_call(
        functools.partial(_attn_out_ffn_kernel, fc=fc), grid=(b, s // tm),
        in_specs=[pl.BlockSpec((1, tm, d), lambda bi, i: (bi, i, 0)),
                  pl.BlockSpec((1, hv, tm), lambda bi, i: (bi, 0, i)),
                  _resident(wo.shape), _resident(g.shape), _resident(wg.shape), _resident(wu.shape),
                  _resident(wd.shape)],
        out_specs=pl.BlockSpec((1, tm, d), lambda bi, i: (bi, i, 0)),
        out_shape=jax.ShapeDtypeStruct((b, s, d), F32),
        scratch_shapes=[pltpu.VMEM((tm, f), BF16)],
        compiler_params=_cparams("parallel", "parallel"), name="attn_out_ffn",
    )(x, ot, wo, g, wg, wu, wd)


def _s5_param_kernel(are_ref, aim_ref, ls_ref, bre_ref, bim_ref, lr_ref, li_ref, bbr_ref, bbi_ref):
    ar = are_ref[...]
    ai = aim_ref[...]
    delta = jnp.exp(ls_ref[...])
    mag = jnp.exp(ar * delta)
    lr = mag * jnp.cos(ai * delta)
    li = mag * jnp.sin(ai * delta)
    nr = lr - 1.0
    den = ar * ar + ai * ai
    cr = (nr * ar + li * ai) / den
    ci = (li * ar - nr * ai) / den
    br = bre_ref[...]
    bi = bim_ref[...]
    lr_ref[...] = lr
    li_ref[...] = li
    bbr_ref[...] = cr * br - ci * bi
    bbi_ref[...] = cr * bi + ci * br


def _s5_scan_kernel(x_ref, g_ref, bt_ref, ct_ref, lr_ref, li_ref, init_ref, *rest, tt, reverse, emit_y):
    if emit_y:
        y_ref, fin_ref, bu0_ref, bu1_ref, st_ref = rest
    else:
        fin_ref, bu0_ref, bu1_ref, st_ref = rest
    c = pl.program_id(1)

    @pl.when(c == 0)
    def _():
        st_ref[...] = init_ref[0]

    hb = _rms(x_ref[0], g_ref[...]).astype(BF16)

    def write_y(j, y):
        y_ref[0, :, j * MXU_DIM:(j + 1) * MXU_DIM] = y

    _s5_direction(hb, bt_ref, ct_ref, lr_ref, li_ref, st_ref, (bu0_ref, bu1_ref), tt, reverse,
                  write_y if emit_y else None)

    @pl.when(c == pl.num_programs(1) - 1)
    def _():
        fin_ref[0] = st_ref[...]


def _s5_direction(hb, bt_ref, ct_ref, lr_ref, li_ref, st_ref, bu_refs, tt, reverse, write_y):
    for j in range(bt_ref.shape[0]):
        _s5_tile(hb, j, bt_ref, ct_ref, lr_ref, li_ref, st_ref, bu_refs[j % 2], tt, reverse, write_y)


def _s5_tile(hb, j, bt_ref, ct_ref, lr_ref, li_ref, st_ref, bu_ref, tt, reverse, write_y):
    half = bt_ref.shape[2] // 2
    bu_ref[...] = _dot(hb[:, j * MXU_DIM:(j + 1) * MXU_DIM], bt_ref[j])
    lr = jnp.broadcast_to(lr_ref[j], (SUBLANES, half))
    li = jnp.broadcast_to(li_ref[j], (SUBLANES, half))
    xr, xi = st_ref[j, 0], st_ref[j, 1]
    for t in range(tt):
        r0 = ((tt - 1 - t) if reverse else t) * SUBLANES
        nr = lr * xr - li * xi + bu_ref[r0:r0 + SUBLANES, 0:half]
        ni = lr * xi + li * xr + bu_ref[r0:r0 + SUBLANES, half:2 * half]
        bu_ref[r0:r0 + SUBLANES, 0:half] = nr
        bu_ref[r0:r0 + SUBLANES, half:2 * half] = ni
        xr, xi = nr, ni
    st_ref[j, 0] = xr
    st_ref[j, 1] = xi
    if write_y is not None:
        write_y(j, _dot(bu_ref[...].astype(BF16), ct_ref[j]))


def _s5_glu_router_kernel(x_ref, yf_ref, yb_ref, gm_ref, d_ref, wglu_ref, gf_ref, wr_ref,
                          x1_ref, h2_ref, idx_ref, gate_ref):
    tm = x_ref.shape[0]
    sub = tm // 2 if tm % (2 * SUBLANES) == 0 else tm
    for r0 in range(0, tm, sub):
        rows = slice(r0, r0 + sub)
        x1, h2, idx, gate = _glu_router_rows(x_ref[rows, :], yf_ref[rows, :] + yb_ref[rows, :], gm_ref, d_ref,
                                             wglu_ref, gf_ref, wr_ref)
        x1_ref[rows, :] = x1
        h2_ref[rows, :] = h2
        idx_ref[rows, :] = idx
        gate_ref[rows, :] = gate


def _glu_router_rows(x, ysum, gm_ref, d_ref, wglu_ref, gf_ref, wr_ref):
    d = x.shape[1]
    y = ysum + d_ref[...] * _rms(x, gm_ref[...])
    g = jax.nn.gelu(y, approximate=True).astype(BF16)
    z = _dot(g, wglu_ref[...])
    x1 = x + z[:, :d] * jax.nn.sigmoid(z[:, d:])
    h2 = _rms(x1, gf_ref[...])
    hi = h2.astype(BF16)
    lo = (h2 - hi.astype(F32)).astype(BF16)
    logits = _dot(hi, wr_ref[0]) + (_dot(lo, wr_ref[0]) + _dot(hi, wr_ref[1]))
    idx, gate = _topk2(logits)
    return x1, h2, idx, gate


def _s5_glu_router(x, yf, yb, gm, dskip, wglu, gf, wr, tm):
    t, d = x.shape
    row = pl.BlockSpec((tm, d), lambda i: (i, 0))
    nar = pl.BlockSpec((tm, LANES), lambda i: (i, 0))
    return pl.pallas_call(
        _s5_glu_router_kernel, grid=(t // tm,),
        in_specs=[row, row, row, _resident(gm.shape), _resident(dskip.shape), _resident(wglu.shape),
                  _resident(gf.shape), _resident(wr.shape)],
        out_specs=[row, row, nar, nar],
        out_shape=[jax.ShapeDtypeStruct((t, d), F32), jax.ShapeDtypeStruct((t, d), F32),
                   jax.ShapeDtypeStruct((t, LANES), jnp.int32), jax.ShapeDtypeStruct((t, LANES), F32)],
        compiler_params=_cparams("parallel"), name="s5_glu_router",
    )(x, yf, yb, gm, dskip, wglu, gf, wr)


def _s5_scan(xp, g, bt, ct, lr, li, init, tt, reverse, emit_y):
    nrg, rows, d = xp.shape
    s = rows // SUBLANES
    nc = s // tt
    ntile, _, sw = bt.shape
    half = sw // 2
    blk = tt * SUBLANES
    if reverse:
        xmap = lambda gi, c: (gi, nc - 1 - c, 0)
    else:
        xmap = lambda gi, c: (gi, c, 0)
    st_shape = (ntile, 2, SUBLANES, half)
    fin_spec = pl.BlockSpec((1,) + st_shape, lambda gi, c: (gi, 0, 0, 0, 0))
    fin_shape = jax.ShapeDtypeStruct((nrg,) + st_shape, F32)
    if emit_y:
        out_specs = [pl.BlockSpec((1, blk, d), xmap), fin_spec]
        out_shape = [jax.ShapeDtypeStruct(xp.shape, F32), fin_shape]
    else:
        out_specs = [fin_spec]
        out_shape = [fin_shape]
    outs = pl.pallas_call(
        functools.partial(_s5_scan_kernel, tt=tt, reverse=reverse, emit_y=emit_y), grid=(nrg, nc),
        in_specs=[pl.BlockSpec((1, blk, d), xmap), _resident(g.shape), _resident(bt.shape),
                  _resident(ct.shape), _resident(lr.shape), _resident(li.shape),
                  pl.BlockSpec((1,) + st_shape, lambda gi, c: (gi, 0, 0, 0, 0))],
        out_specs=out_specs, out_shape=out_shape,
        scratch_shapes=[pltpu.VMEM((blk, sw), F32), pltpu.VMEM((blk, sw), F32), pltpu.VMEM(st_shape, F32)],
        compiler_params=_cparams("parallel", "arbitrary"),
        name="s5_scan_" + ("bwd" if reverse else "fwd") + ("" if emit_y else "_state"),
    )(xp, g, bt, ct, lr, li, init)
    if emit_y:
        return outs[0], outs[1]
    return None, outs[0]


def _topk2(logits):
    lane = lax.broadcasted_iota(jnp.int32, logits.shape, 1)
    lg = jnp.where(lane < N_EXPERTS, logits, -jnp.inf)
    m1 = jnp.max(lg, axis=-1, keepdims=True)
    i1 = jnp.min(jnp.where(lg == m1, lane, LANES), axis=-1, keepdims=True)
    lg2 = jnp.where(lane == i1, -jnp.inf, lg)
    m2 = jnp.max(lg2, axis=-1, keepdims=True)
    i2 = jnp.min(jnp.where(lg2 == m2, lane, LANES), axis=-1, keepdims=True)
    e = jnp.exp(m2 - m1)
    g1 = 1.0 / (1.0 + e)
    g2 = e / (1.0 + e)
    idx = jnp.where(lane == 0, i1, jnp.where(lane == 1, i2, 0))
    gate = jnp.where(lane == 0, g1, jnp.where(lane == 1, g2, 0.0))
    return idx, gate


def _moe_plan_kernel(idx_ref, ltri_ref, rank_ref, cnt_ref, carry_ref):
    @pl.when(pl.program_id(0) == 0)
    def _():
        carry_ref[...] = jnp.zeros_like(carry_ref)

    idx = idx_ref[...]
    lane = lax.broadcasted_iota(jnp.int32, idx.shape, 1)
    oh1 = lane == idx[:, 0:1]
    oh2 = lane == idx[:, 1:2]
    oh = jnp.where(oh1, 1.0, jnp.where(oh2, 1.0, 0.0)).astype(BF16)
    cs = _dot(ltri_ref[...], oh)
    tot = carry_ref[...] + cs - 1.0
    r1 = jnp.sum(jnp.where(oh1, tot, 0.0), axis=-1, keepdims=True)
    r2 = jnp.sum(jnp.where(oh2, tot, 0.0), axis=-1, keepdims=True)
    rank_ref[...] = jnp.where(lane == 0, r1, jnp.where(lane == 1, r2, 0.0)).astype(jnp.int32)
    new = carry_ref[...] + cs[cs.shape[0] - 1:, :]
    carry_ref[...] = new
    cnt_ref[...] = new


def _moe_plan(idx, tm):
    t = idx.shape[0]
    ltri = jnp.tril(jnp.ones((tm, tm), F32)).astype(BF16)
    nar = pl.BlockSpec((tm, LANES), lambda i: (i, 0))
    return pl.pallas_call(
        _moe_plan_kernel, grid=(t // tm,),
        in_specs=[nar, _resident(ltri.shape)],
        out_specs=[nar, pl.BlockSpec((1, LANES), lambda i: (0, 0))],
        out_shape=[jax.ShapeDtypeStruct((t, LANES), jnp.int32), jax.ShapeDtypeStruct((1, LANES), F32)],
        scratch_shapes=[pltpu.VMEM((1, LANES), F32)],
        compiler_params=_cparams("arbitrary"), name="moe_plan",
    )(idx, ltri)


ROW_UNROLL = 8


def _row_copy(src, dst, i, j, sem):
    return pltpu.make_async_copy(src.at[pl.ds(i, 1)], dst.at[pl.ds(j, 1)], sem)


def _moe_dispatch_kernel(pos_ref, x_ref, xs_in, xs_hbm, sem, *, tm):
    del xs_in

    def issue(g, c):
        for u in range(ROW_UNROLL):
            r = g * ROW_UNROLL + u
            _row_copy(x_ref, xs_hbm, r, pos_ref[0, 0, 2 * r], sem).start(priority=0)
            _row_copy(x_ref, xs_hbm, r, pos_ref[0, 0, 2 * r + 1], sem).start(priority=1)
        return c

    lax.fori_loop(0, tm // ROW_UNROLL, issue, 0)

    def drain(g, c):
        for _ in range(2 * ROW_UNROLL):
            _row_copy(x_ref, xs_hbm, 0, 0, sem).wait()
        return c

    lax.fori_loop(0, tm // ROW_UNROLL, drain, 0)


def _moe_dispatch(h2, pos, rows, tm):
    t, d = h2.shape
    pos3 = pos.reshape(t // tm, 1, 2 * tm)
    xs0 = jnp.zeros((rows, d), F32)
    return pl.pallas_call(
        functools.partial(_moe_dispatch_kernel, tm=tm), grid=(t // tm,),
        in_specs=[pl.BlockSpec((1, 1, 2 * tm), lambda i: (i, 0, 0), memory_space=pltpu.SMEM),
                  pl.BlockSpec((tm, d), lambda i: (i, 0)), pl.BlockSpec(memory_space=pl.ANY)],
        out_specs=pl.BlockSpec(memory_space=pl.ANY),
        out_shape=jax.ShapeDtypeStruct((rows, d), F32),
        scratch_shapes=[pltpu.SemaphoreType.DMA(())],
        input_output_aliases={2: 0},
        compiler_params=_cparams("arbitrary"), name="moe_dispatch",
    )(pos3, h2, xs0)


def _moe_expert_kernel(te_ref, nu_ref, xs_ref, wg_ref, wu_ref, wd_ref, y_ref, act_ref, *, fc):
    i = pl.program_id(0)

    @pl.when(i < nu_ref[0])
    def _():
        y_ref[...] = _swiglu(xs_ref[...].astype(BF16), wg_ref.at[0], wu_ref.at[0], wd_ref.at[0], act_ref, fc)

    @pl.when(i >= nu_ref[0])
    def _():
        y_ref[...] = jnp.zeros_like(y_ref)


def _moe_experts(xs, tile_expert, n_used, wg, wu, wd, tm):
    rows, d = xs.shape
    f = wg.shape[2]
    fc = _tile(f, 1792, LANES)
    wspec = lambda shp: pl.BlockSpec((1,) + shp, lambda i, te, nu: (te[i], 0, 0), pipeline_mode=pl.Buffered(1))
    gs = pltpu.PrefetchScalarGridSpec(
        num_scalar_prefetch=2, grid=(rows // tm,),
        in_specs=[pl.BlockSpec((tm, d), lambda i, te, nu: (i, 0)),
                  wspec((d, f)), wspec((d, f)), wspec((f, d))],
        out_specs=pl.BlockSpec((tm, d), lambda i, te, nu: (i, 0)),
        scratch_shapes=[pltpu.VMEM((tm, f), BF16)])
    return pl.pallas_call(
        functools.partial(_moe_expert_kernel, fc=fc), grid_spec=gs,
        out_shape=jax.ShapeDtypeStruct((rows, d), F32),
        compiler_params=_cparams("arbitrary"), name="moe_experts",
    )(tile_expert, n_used, xs, wg, wu, wd)


def _moe_combine_kernel(pos_ref, x_ref, gate_ref, ys_hbm, gf_ref, o_ref, b0_ref, b1_ref, sem, *, tm, final_norm):
    def issue(g, c):
        for u in range(ROW_UNROLL):
            r = g * ROW_UNROLL + u
            _row_copy(ys_hbm, b0_ref, pos_ref[0, 0, 2 * r], r, sem).start(priority=0)
            _row_copy(ys_hbm, b1_ref, pos_ref[0, 0, 2 * r + 1], r, sem).start(priority=1)
        return c

    lax.fori_loop(0, tm // ROW_UNROLL, issue, 0)

    def drain(g, c):
        for _ in range(ROW_UNROLL):
            _row_copy(ys_hbm, b0_ref, 0, 0, sem).wait()
            _row_copy(ys_hbm, b1_ref, 0, 0, sem).wait()
        return c

    lax.fori_loop(0, tm // ROW_UNROLL, drain, 0)
    gate = gate_ref[...]
    out = x_ref[...] + gate[:, 0:1] * b0_ref[...] + gate[:, 1:2] * b1_ref[...]
    if final_norm:
        out = _rms(out, gf_ref[...])
    o_ref[...] = out


def _moe_combine(x1, gate, ys, pos, gfin, tm, final_norm):
    t, d = x1.shape
    pos3 = pos.reshape(t // tm, 1, 2 * tm)
    return pl.pallas_call(
        functools.partial(_moe_combine_kernel, tm=tm, final_norm=final_norm), grid=(t // tm,),
        in_specs=[pl.BlockSpec((1, 1, 2 * tm), lambda i: (i, 0, 0), memory_space=pltpu.SMEM),
                  pl.BlockSpec((tm, d), lambda i: (i, 0)),
                  pl.BlockSpec((tm, LANES), lambda i: (i, 0)),
                  pl.BlockSpec(memory_space=pl.ANY), _resident(gfin.shape)],
        out_specs=pl.BlockSpec((tm, d), lambda i: (i, 0)),
        out_shape=jax.ShapeDtypeStruct((t, d), F32),
        scratch_shapes=[pltpu.VMEM((tm, d), F32), pltpu.VMEM((tm, d), F32), pltpu.SemaphoreType.DMA(())],
        compiler_params=_cparams("arbitrary"), name="moe_combine",
    )(pos3, x1, gate, ys, gfin)


def _rmsnorm_kernel(x_ref, g_ref, o_ref):
    o_ref[...] = _rms(x_ref[...], g_ref[...])


def _rmsnorm(x, g, tm):
    t, d = x.shape
    return pl.pallas_call(
        _rmsnorm_kernel, grid=(t // tm,),
        in_specs=[pl.BlockSpec((tm, d), lambda i: (i, 0)), _resident(g.shape)],
        out_specs=pl.BlockSpec((tm, d), lambda i: (i, 0)),
        out_shape=jax.ShapeDtypeStruct((t, d), F32),
        compiler_params=_cparams("parallel"), name="final_norm",
    )(x, g)


def _rot_cols(w):
    half = QK_ROPE // 2
    return jnp.concatenate([-w[..., half:], w[..., :half]], axis=-1)


def _prep_mla(w_dq, q_norm, w_uq, w_dkv, kv_norm, w_ukv, w_o):
    ql = w_dq.shape[1]
    kvl = w_ukv.shape[0]
    wq = w_uq.reshape(ql, N_HEADS, QK_NOPE + QK_ROPE)
    wq_nope, wq_rope = wq[..., :QK_NOPE], wq[..., QK_NOPE:]
    zpad = jnp.zeros((ql, N_HEADS, HEAD_PAD - QK_NOPE - QK_ROPE), F32)
    w1 = jnp.concatenate([wq_nope, wq_rope, zpad], axis=-1).reshape(ql, N_HEADS * HEAD_PAD)
    w2 = _rot_cols(wq_rope).reshape(ql, N_HEADS * QK_ROPE)
    kr_w = w_dkv[:, kvl:]
    lpad = jnp.zeros((w_dkv.shape[0], LANES - QK_ROPE), F32)
    wdkv = jnp.concatenate([w_dkv[:, :kvl], kr_w, lpad, _rot_cols(kr_w), lpad], axis=1)
    wkv = w_ukv.reshape(kvl, N_HEADS, QK_NOPE + V_DIM)
    wk_top = jnp.concatenate([wkv[..., :QK_NOPE], jnp.zeros((kvl, N_HEADS, HEAD_PAD - QK_NOPE), F32)], axis=-1)
    eye = jnp.eye(LANES, QK_ROPE, dtype=F32)
    wk_bot = jnp.concatenate([jnp.zeros((LANES, QK_NOPE), F32), eye,
                              jnp.zeros((LANES, HEAD_PAD - QK_NOPE - QK_ROPE), F32)], axis=-1)
    wk_bot = jnp.broadcast_to(wk_bot[:, None, :], (LANES, N_HEADS, HEAD_PAD))
    wk = jnp.concatenate([wk_top, wk_bot], axis=0).reshape(kvl + LANES, N_HEADS * HEAD_PAD)
    wvt = wkv[..., QK_NOPE:].reshape(kvl, N_HEADS * V_DIM).T
    return dict(wdq=w_dq.astype(BF16), qn=q_norm.reshape(1, -1), w1t=w1.T.astype(BF16), w2t=w2.T.astype(BF16),
                wdkv=wdkv.astype(BF16), kvn=kv_norm.reshape(1, -1), wk=wk.astype(BF16), wvt=wvt.astype(BF16),
                wo=w_o.astype(BF16))


def _rope_tables(s):
    pos = jnp.arange(s, dtype=F32)
    inv = ROPE_THETA ** (-jnp.arange(0, QK_ROPE, 2, dtype=F32) / QK_ROPE)
    ang = pos[:, None] * inv[None, :]
    ang = jnp.concatenate([ang, ang], axis=-1)
    cos, sin = jnp.cos(ang), jnp.sin(ang)
    qs = ATTN_SCALE * LOG2E
    zq = jnp.zeros((s, HEAD_PAD - QK_NOPE - QK_ROPE), F32)
    ctq = jnp.concatenate([jnp.full((s, QK_NOPE), qs, F32), cos * qs, zq], axis=1).T
    stq = (sin * qs).T
    zk = jnp.zeros((s, LANES - QK_ROPE), F32)
    ctk = jnp.concatenate([cos, zk], axis=1)
    stk = jnp.concatenate([sin, zk], axis=1)
    return ctq, stq, ctk, stk


def _prep_s5(a_re, a_im, log_step, b_re, b_im, c_re, c_im):
    _, g, p = a_re.shape
    n = g * p * GROUP_CH
    rep = lambda a: jnp.broadcast_to(a[..., None], (2, g, p, GROUP_CH)).reshape(2, n)
    ls = jnp.broadcast_to(log_step[:, :, None, None], (2, g, p, GROUP_CH)).reshape(2, n)
    outs = pl.pallas_call(
        _s5_param_kernel,
        out_shape=[jax.ShapeDtypeStruct((2, n), F32)] * 4, name="s5_params",
    )(rep(a_re), rep(a_im), ls, b_re.reshape(2, n), b_im.reshape(2, n))
    lr, li, bbr, bbi = [o.reshape(2, g, p, GROUP_CH) for o in outs]
    lr, li = lr[..., 0], li[..., 0]
    gpt = MXU_DIM // GROUP_CH
    ntile = g // gpt
    eye = jnp.eye(gpt, dtype=F32)

    def tiles_b(bb):
        t = bb.reshape(2, ntile, gpt, p, GROUP_CH)
        return jnp.einsum('djapc,ab->djacbp', t, eye).reshape(2, ntile, gpt * GROUP_CH, gpt * p)

    def tiles_c(cc):
        t = cc.reshape(2, ntile, gpt, GROUP_CH, p)
        return jnp.einsum('djacp,ab->djapbc', t, eye).reshape(2, ntile, gpt * p, gpt * GROUP_CH)

    bt = jnp.concatenate([tiles_b(bbr), tiles_b(bbi)], axis=-1).astype(BF16)
    ct = jnp.concatenate([tiles_c(c_re), tiles_c(-c_im)], axis=-2).astype(BF16)
    lrt = lr.reshape(2, ntile, 1, gpt * p)
    lit = li.reshape(2, ntile, 1, gpt * p)
    return bt, ct, lrt, lit


def _split_bf16(w):
    hi = w.astype(BF16)
    lo = (w - hi.astype(F32)).astype(BF16)
    return jnp.stack([hi, lo])


def _cpow(lr, li, n):
    k = int(math.log2(n))
    assert 2 ** k == n
    for _ in range(k):
        lr, li = lr * lr - li * li, 2.0 * lr * li
    return lr, li


def _mla_layer(x, p, j, tabs, tm, attn_cfg):
    w = p['mla'][j]
    qt, k, vt = _mla_pre(x, p['norm_mix'][2 * j], w, tabs, tm)
    ot = _attention(qt, k, vt, *attn_cfg)
    f = p['ffn'][j]
    return _attn_out_ffn(x, ot, w['wo'], p['norm_ffn'][2 * j], f['wg'], f['wu'], f['wd'], tm)


def _s5_states(xp, g, s5, nseg, tt):
    bt, ct, lr, li = s5
    nrg = xp.shape[0]
    ntile, half = bt.shape[1], bt.shape[3] // 2
    zero = jnp.zeros((nrg, ntile, 2, SUBLANES, half), F32)
    if nseg == 1:
        return zero, zero
    seg_len = xp.shape[1] // SUBLANES
    inits = []
    for k, reverse in ((0, False), (1, True)):
        _, fin = _s5_scan(xp, g, bt[k], ct[k], lr[k], li[k], zero, tt, reverse, False)
        e = fin.reshape(nrg, ntile, 2, SUBLANES // nseg, nseg, half)
        plr, pli = _cpow(lr[k], li[k], seg_len)
        plr, pli = plr[None], pli[None]
        order = range(nseg - 1, -1, -1) if reverse else range(nseg)
        sr = jnp.zeros_like(e[:, :, 0, :, 0])
        si = jnp.zeros_like(sr)
        init_r = [None] * nseg
        init_i = [None] * nseg
        for q in order:
            init_r[q], init_i[q] = sr, si
            er, ei = e[:, :, 0, :, q], e[:, :, 1, :, q]
            sr, si = plr * sr - pli * si + er, plr * si + pli * sr + ei
        init = jnp.stack([jnp.stack(init_r, axis=3), jnp.stack(init_i, axis=3)], axis=2)
        inits.append(init.reshape(nrg, ntile, 2, SUBLANES, half))
    return inits[0], inits[1]


def _s5_moe_layer(xp, p, j, nseg, tt, tm, last):
    nrg, rows, d = xp.shape
    i = 2 * j + 1
    s5 = p['s5'][j]
    bt, ct, lr, li = s5
    gm = p['norm_mix'][i]
    init_f, init_b = _s5_states(xp, gm, s5, nseg, tt)
    yf, _ = _s5_scan(xp, gm, bt[0], ct[0], lr[0], li[0], init_f, tt, False, True)
    yb, _ = _s5_scan(xp, gm, bt[1], ct[1], lr[1], li[1], init_b, tt, True, True)
    t = nrg * rows
    m = p['moe'][j]
    x1, h2, idx, gate = _s5_glu_router(xp.reshape(t, d), yf.reshape(t, d), yb.reshape(t, d), gm,
                                       p['s5_d'][j], p['s5_wglu'][j], p['norm_ffn'][i], m['wr'], tm)
    rank, cnt = _moe_plan(idx, tm)
    counts = cnt[0, :N_EXPERTS].astype(jnp.int32)
    ntiles = (counts + tm - 1) // tm
    tile_end = jnp.cumsum(ntiles)
    offs = (tile_end - ntiles) * tm
    e2 = idx[:, :2]
    pos = offs[e2] + rank[:, :2]
    nt_max = (2 * t) // tm + N_EXPERTS
    tile_expert = jnp.minimum(jnp.searchsorted(tile_end, jnp.arange(nt_max, dtype=jnp.int32), side='right'),
                              N_EXPERTS - 1).astype(jnp.int32)
    n_used = tile_end[-1:].astype(jnp.int32)
    xs = _moe_dispatch(h2, pos, nt_max * tm, tm)
    ys = _moe_experts(xs, tile_expert, n_used, m['wg'], m['wu'], m['wd'], tm)
    out = _moe_combine(x1, gate, ys, pos, p['norm_final'], tm, last)
    return out.reshape(nrg, rows, d)


def _to_perm(x, nseg):
    b, s, d = x.shape
    nrg = b * nseg // SUBLANES
    xr = x.reshape(nrg, SUBLANES, s // nseg, d)
    return jnp.swapaxes(xr, 1, 2).reshape(nrg, (s // nseg) * SUBLANES, d)


def _from_perm(xp, b, s, nseg):
    nrg, rows, d = xp.shape
    xr = xp.reshape(nrg, rows // SUBLANES, SUBLANES, d)
    return jnp.swapaxes(xr, 1, 2).reshape(b, s, d)


def _trunk(x, p):
    b, s, d = x.shape
    tm = _tile(s, 512)
    long_seq = s > 4096
    tq = _tile(s, 512 if long_seq else 2048, LANES)
    tk = _tile(s, min(512 if long_seq else 256, s // 4), LANES)
    attn_cfg = (tq, tk, 4 if long_seq else 2, 2)
    nseg = 1 if b % SUBLANES == 0 else SUBLANES // b
    assert (b * nseg) % SUBLANES == 0 and s % nseg == 0
    tt = _tile(s // nseg, 64)
    tabs = _rope_tables(s)
    depth = p['norm_mix'].shape[0]
    for i in range(depth):
        j = i // 2
        if i % 2 == 0:
            x = _mla_layer(x, p, j, tabs, tm, attn_cfg)
        else:
            xp = _s5_moe_layer(_to_perm(x, nseg), p, j, nseg, tt, tm, last=(i == depth - 1))
            x = _from_perm(xp, b, s, nseg)
    if depth % 2 == 1:
        x = _rmsnorm(x.reshape(b * s, d), p['norm_final'], tm).reshape(b, s, d)
    return x


def kernel(x_prompt, x_sample, norm_mix, norm_ffn, norm_final, mla_w_dq, mla_q_norm, mla_w_uq, mla_w_dkv,
           mla_kv_norm, mla_w_ukv, mla_w_o, ssm_a_re, ssm_a_im, ssm_log_step, ssm_b_re, ssm_b_im, ssm_c_re,
           ssm_c_im, ssm_d, ssm_w_glu, ffn_w_gate, ffn_w_up, ffn_w_down, moe_w_router, moe_w_gate, moe_w_up,
           moe_w_down):
    d = x_prompt.shape[-1]
    na, ns = mla_w_dq.shape[0], ssm_a_re.shape[0]
    p = dict(
        norm_mix=norm_mix.reshape(-1, 1, d), norm_ffn=norm_ffn.reshape(-1, 1, d), norm_final=norm_final.reshape(1, d),
        mla=[_prep_mla(mla_w_dq[j], mla_q_norm[j], mla_w_uq[j], mla_w_dkv[j], mla_kv_norm[j], mla_w_ukv[j],
                       mla_w_o[j]) for j in range(na)],
        ffn=[dict(wg=ffn_w_gate[j].astype(BF16), wu=ffn_w_up[j].astype(BF16), wd=ffn_w_down[j].astype(BF16))
             for j in range(na)],
        s5=[_prep_s5(ssm_a_re[j], ssm_a_im[j], ssm_log_step[j], ssm_b_re[j], ssm_b_im[j], ssm_c_re[j], ssm_c_im[j])
            for j in range(ns)],
        s5_d=[ssm_d[j].reshape(1, d) for j in range(ns)],
        s5_wglu=[ssm_w_glu[j].astype(BF16) for j in range(ns)],
        moe=[dict(wr=_split_bf16(jnp.pad(moe_w_router[j], ((0, 0), (0, LANES - N_EXPERTS)))),
                  wg=moe_w_gate[j].astype(BF16), wu=moe_w_up[j].astype(BF16), wd=moe_w_down[j].astype(BF16))
             for j in range(ns)],
    )
    return (_trunk(x_prompt, p), _trunk(x_sample, p))
```

```python
import functools
import math

import jax
import jax.numpy as jnp
from jax import lax
from jax.experimental import pallas as pl
from jax.experimental.pallas import tpu as pltpu

BF16 = jnp.bfloat16
F32 = jnp.float32

N_HEADS = 16
QK_NOPE = 64
QK_ROPE = 32
V_DIM = 64
HEAD_PAD = 128
ROPE_THETA = 10000.0
ATTN_SCALE = (QK_NOPE + QK_ROPE) ** -0.5
LOG2E = 1.4426950408889634
GROUP_CH = 16
STATE_DIM = 64
N_EXPERTS = 8
NORM_EPS = 1e-6
LANES = 128
SUBLANES = 8
MXU_DIM = 256
VMEM_LIMIT = 56 * 1024 * 1024
NEG_BIG = -1e30


def _cparams(*sem):
    return pltpu.CompilerParams(dimension_semantics=sem, vmem_limit_bytes=VMEM_LIMIT)


def _resident(shape):
    zeros = (0,) * len(shape)
    return pl.BlockSpec(shape, lambda *_: zeros, pipeline_mode=pl.Buffered(1))


def _tile(n, cap, mult=SUBLANES):
    if n <= cap:
        return n
    t = (cap // mult) * mult
    while t > mult and n % t:
        t -= mult
    assert n % t == 0, (n, cap, mult)
    return t


def _rms(x, g):
    ms = jnp.mean(x * x, axis=-1, keepdims=True)
    return x * lax.rsqrt(ms + NORM_EPS) * g


def _dot(a, b):
    return jnp.dot(a, b, preferred_element_type=F32)


def _dot_nt(a, b):
    return lax.dot_general(a, b, (((1,), (1,)), ((), ())), preferred_element_type=F32)


def _dot_tn(a, b):
    return lax.dot_general(a, b, (((0,), (0,)), ((), ())), preferred_element_type=F32)


def _swiglu(h, wg_ref, wu_ref, wd_ref, act_ref, fc):
    f = act_ref.shape[1]
    for c0 in range(0, f, fc):
        g = _dot(h, wg_ref[:, c0:c0 + fc])
        u = _dot(h, wu_ref[:, c0:c0 + fc])
        act_ref[:, c0:c0 + fc] = (g * jax.nn.sigmoid(g) * u).astype(BF16)
    return _dot(act_ref[...], wd_ref[...])


def _mla_pre_kernel(x_ref, g_ref, wdq_ref, qn_ref, w1_ref, w2_ref, wdkv_ref, kvn_ref, wk_ref, wv_ref,
                    ctq_ref, stq_ref, ctk_ref, stk_ref, qt_ref, k_ref, vt_ref):
    h = _rms(x_ref[0], g_ref[...]).astype(BF16)
    cq = _rms(_dot(h, wdq_ref[...]), qn_ref[...]).astype(BF16)
    a = _dot_nt(w1_ref[...], cq)
    b = _dot_nt(w2_ref[...], cq)
    ct = ctq_ref[...]
    st = stq_ref[...]
    r0, r1 = QK_NOPE, QK_NOPE + QK_ROPE
    for hh in range(N_HEADS):
        ah = a[hh * HEAD_PAD:(hh + 1) * HEAD_PAD] * ct
        rope = ah[r0:r1] + b[hh * QK_ROPE:(hh + 1) * QK_ROPE] * st
        qt_ref[0, hh * HEAD_PAD:(hh + 1) * HEAD_PAD, :] = jnp.concatenate([ah[:r0], rope, ah[r1:]], axis=0).astype(BF16)
    ck = _dot(h, wdkv_ref[...])
    kvl = kvn_ref.shape[1]
    ckv = _rms(ck[:, :kvl], kvn_ref[...]).astype(BF16)
    kr = (ck[:, kvl:kvl + LANES] * ctk_ref[...] + ck[:, kvl + LANES:] * stk_ref[...]).astype(BF16)
    kaug = jnp.concatenate([ckv, kr], axis=1)
    k_ref[0] = _dot(kaug, wk_ref[...]).astype(BF16)
    vt_ref[0] = _dot_nt(wv_ref[...], ckv).astype(BF16)


def _mla_pre(x, g, w, tabs, tm):
    b, s, d = x.shape
    hq = N_HEADS * HEAD_PAD
    hv = N_HEADS * V_DIM
    ctq, stq, ctk, stk = tabs
    grid = (b, s // tm)
    in_specs = [
        pl.BlockSpec((1, tm, d), lambda bi, i: (bi, i, 0)),
        _resident(g.shape), _resident(w['wdq'].shape), _resident(w['qn'].shape),
        _resident(w['w1t'].shape), _resident(w['w2t'].shape), _resident(w['wdkv'].shape),
        _resident(w['kvn'].shape), _resident(w['wk'].shape), _resident(w['wvt'].shape),
        pl.BlockSpec((HEAD_PAD, tm), lambda bi, i: (0, i)),
        pl.BlockSpec((QK_ROPE, tm), lambda bi, i: (0, i)),
        pl.BlockSpec((tm, LANES), lambda bi, i: (i, 0)),
        pl.BlockSpec((tm, LANES), lambda bi, i: (i, 0)),
    ]
    out_specs = [
        pl.BlockSpec((1, hq, tm), lambda bi, i: (bi, 0, i)),
        pl.BlockSpec((1, tm, hq), lambda bi, i: (bi, i, 0)),
        pl.BlockSpec((1, hv, tm), lambda bi, i: (bi, 0, i)),
    ]
    out_shape = [jax.ShapeDtypeStruct((b, hq, s), BF16), jax.ShapeDtypeStruct((b, s, hq), BF16),
                 jax.ShapeDtypeStruct((b, hv, s), BF16)]
    return pl.pallas_call(
        _mla_pre_kernel, grid=grid, in_specs=in_specs, out_specs=out_specs, out_shape=out_shape,
        compiler_params=_cparams("parallel", "parallel"), name="mla_pre",
    )(x, g, w['wdq'], w['qn'], w['w1t'], w['w2t'], w['wdkv'], w['kvn'], w['wk'], w['wvt'], ctq, stq, ctk, stk)


SUM_ROWS = 16


def _attn_kernel(qt_ref, k_ref, vt_ref, o_ref, *rest, tk, unroll):
    s_refs, (acc_ref, m_ref) = rest[:-2], rest[-2:]
    nbuf = len(s_refs)
    qt = qt_ref[0]
    nk = k_ref.shape[1] // tk
    ones = jnp.ones((SUM_ROWS, tk), BF16)

    def scores(j, dst_ref):
        k0 = pl.multiple_of(j * tk, tk)
        dst_ref[...] = _dot(k_ref[0, pl.ds(k0, tk), :], qt)

    def update(j, src_ref):
        k0 = pl.multiple_of(j * tk, tk)
        s = src_ref[...]
        m_old = m_ref[...]
        m_new = jnp.maximum(m_old, jnp.max(s, axis=0, keepdims=True))
        p = jnp.exp2(s - m_new).astype(BF16)
        v = jnp.concatenate([vt_ref[0, :, pl.ds(k0, tk)], ones], axis=0)
        acc_ref[...] = jnp.exp2(m_old - m_new) * acc_ref[...] + _dot(v, p)
        m_ref[...] = m_new

    m_ref[...] = jnp.full(m_ref.shape, NEG_BIG, F32)
    acc_ref[...] = jnp.zeros(acc_ref.shape, F32)
    ahead = nbuf - 1
    for b in range(ahead):
        scores(b, s_refs[b])
    n_full = (nk - 1 - 2 * ahead) // nbuf + 1 if nk > 2 * ahead else 0

    def ring(t, c):
        j = t * nbuf
        for u in range(nbuf):
            scores(j + u + ahead, s_refs[(u + ahead) % nbuf])
            update(j + u, s_refs[u])
        return c

    lax.fori_loop(0, n_full, ring, 0, unroll=unroll)
    for b in range(n_full * nbuf, nk):
        if b + ahead < nk:
            scores(b + ahead, s_refs[(b + ahead) % nbuf])
        update(b, s_refs[b % nbuf])
    acc = acc_ref[...]
    o_ref[0] = (acc[:V_DIM] / acc[V_DIM:V_DIM + 1]).astype(BF16)


def _attention(qt, k, vt, tq, tk, unroll, nbuf):
    b, hq, s = qt.shape
    assert s // tk >= nbuf - 1
    grid = (b, N_HEADS, s // tq)
    return pl.pallas_call(
        functools.partial(_attn_kernel, tk=tk, unroll=unroll), grid=grid,
        in_specs=[pl.BlockSpec((1, HEAD_PAD, tq), lambda bi, h, i: (bi, h, i)),
                  pl.BlockSpec((1, s, HEAD_PAD), lambda bi, h, i: (bi, 0, h)),
                  pl.BlockSpec((1, V_DIM, s), lambda bi, h, i: (bi, h, 0))],
        out_specs=pl.BlockSpec((1, V_DIM, tq), lambda bi, h, i: (bi, h, i)),
        out_shape=jax.ShapeDtypeStruct((b, N_HEADS * V_DIM, s), BF16),
        scratch_shapes=[pltpu.VMEM((tk, tq), F32)] * nbuf + [
            pltpu.VMEM((V_DIM + SUM_ROWS, tq), F32), pltpu.VMEM((1, tq), F32)],
        compiler_params=_cparams("parallel", "parallel", "parallel"), name="mla_attention",
    )(qt, k, vt)


def _attn_out_ffn_kernel(x_ref, ot_ref, wo_ref, g_ref, wg_ref, wu_ref, wd_ref, y_ref, act_ref, *, fc):
    x1 = x_ref[0] + _dot_tn(ot_ref[0], wo_ref[...])
    h = _rms(x1, g_ref[...]).astype(BF16)
    y_ref[0] = x1 + _swiglu(h, wg_ref, wu_ref, wd_ref, act_ref, fc)


def _attn_out_ffn(x, ot, wo, g, wg, wu, wd, tm):
    b, s, d = x.shape
    f = wg.shape[1]
    fc = f
    hv = ot.shape[1]
    return pl.pallas_call(
        functools.partial(_attn_out_ffn_kernel, fc=fc), grid=(b, s // tm),
        in_specs=[pl.BlockSpec((1, tm, d), lambda bi, i: (bi, i, 0)),
                  pl.BlockSpec((1, hv, tm), lambda bi, i: (bi, 0, i)),
                  _resident(wo.shape), _resident(g.shape), _resident(wg.shape), _resident(wu.shape),
                  _resident(wd.shape)],
        out_specs=pl.BlockSpec((1, tm, d), lambda bi, i: (bi, i, 0)),
        out_shape=jax.ShapeDtypeStruct((b, s, d), F32),
        scratch_shapes=[pltpu.VMEM((tm, f), BF16)],
        compiler_params=_cparams("parallel", "parallel"), name="attn_out_ffn",
    )(x, ot, wo, g, wg, wu, wd)


def _s5_param_kernel(are_ref, aim_ref, ls_ref, bre_ref, bim_ref, lr_ref, li_ref, bbr_ref, bbi_ref):
    ar = are_ref[...]
    ai = aim_ref[...]
    delta = jnp.exp(ls_ref[...])
    mag = jnp.exp(ar * delta)
    lr = mag * jnp.cos(ai * delta)
    li = mag * jnp.sin(ai * delta)
    nr = lr - 1.0
    den = ar * ar + ai * ai
    cr = (nr * ar + li * ai) / den
    ci = (li * ar - nr * ai) / den
    br = bre_ref[...]
    bi = bim_ref[...]
    lr_ref[...] = lr
    li_ref[...] = li
    bbr_ref[...] = cr * br - ci * bi
    bbi_ref[...] = cr * bi + ci * br


def _s5_scan_kernel(x_ref, g_ref, bt_ref, ct_ref, lr_ref, li_ref, init_ref, *rest, tt, reverse, emit_y):
    if emit_y:
        y_ref, fin_ref, bu0_ref, bu1_ref, st_ref = rest
    else:
        fin_ref, bu0_ref, bu1_ref, st_ref = rest
    c = pl.program_id(1)

    @pl.when(c == 0)
    def _():
        st_ref[...] = init_ref[0]

    hb = _rms(x_ref[0], g_ref[...]).astype(BF16)

    def write_y(j, y):
        y_ref[0, :, j * MXU_DIM:(j + 1) * MXU_DIM] = y

    _s5_direction(hb, bt_ref, ct_ref, lr_ref, li_ref, st_ref, (bu0_ref, bu1_ref), tt, reverse,
                  write_y if emit_y else None)

    @pl.when(c == pl.num_programs(1) - 1)
    def _():
        fin_ref[0] = st_ref[...]


def _s5_direction(hb, bt_ref, ct_ref, lr_ref, li_ref, st_ref, bu_refs, tt, reverse, write_y):
    for j in range(bt_ref.shape[0]):
        _s5_tile(hb, j, bt_ref, ct_ref, lr_ref, li_ref, st_ref, bu_refs[j % 2], tt, reverse, write_y)


def _s5_tile(hb, j, bt_ref, ct_ref, lr_ref, li_ref, st_ref, bu_ref, tt, reverse, write_y):
    half = bt_ref.shape[2] // 2
    bu_ref[...] = _dot(hb[:, j * MXU_DIM:(j + 1) * MXU_DIM], bt_ref[j])
    lr = jnp.broadcast_to(lr_ref[j], (SUBLANES, half))
    li = jnp.broadcast_to(li_ref[j], (SUBLANES, half))
    xr, xi = st_ref[j, 0], st_ref[j, 1]
    for t in range(tt):
        r0 = ((tt - 1 - t) if reverse else t) * SUBLANES
        nr = lr * xr - li * xi + bu_ref[r0:r0 + SUBLANES, 0:half]
        ni = lr * xi + li * xr + bu_ref[r0:r0 + SUBLANES, half:2 * half]
        bu_ref[r0:r0 + SUBLANES, 0:half] = nr
        bu_ref[r0:r0 + SUBLANES, half:2 * half] = ni
        xr, xi = nr, ni
    st_ref[j, 0] = xr
    st_ref[j, 1] = xi
    if write_y is not None:
        write_y(j, _dot(bu_ref[...].astype(BF16), ct_ref[j]))


def _s5_glu_router_kernel(x_ref, yf_ref, yb_ref, gm_ref, d_ref, wglu_ref, gf_ref, wr_ref,
                          x1_ref, h2_ref, idx_ref, gate_ref):
    tm = x_ref.shape[0]
    sub = tm // 2 if tm % (2 * SUBLANES) == 0 else tm
    for r0 in range(0, tm, sub):
        rows = slice(r0, r0 + sub)
        x1, h2, idx, gate = _glu_router_rows(x_ref[rows, :], yf_ref[rows, :] + yb_ref[rows, :], gm_ref, d_ref,
                                             wglu_ref, gf_ref, wr_ref)
        x1_ref[rows, :] = x1
        h2_ref[rows, :] = h2
        idx_ref[rows, :] = idx
        gate_ref[rows, :] = gate


def _glu_router_rows(x, ysum, gm_ref, d_ref, wglu_ref, gf_ref, wr_ref):
    d = x.shape[1]
    y = ysum + d_ref[...] * _rms(x, gm_ref[...])
    g = jax.nn.gelu(y, approximate=True).astype(BF16)
    z = _dot(g, wglu_ref[...])
    x1 = x + z[:, :d] * jax.nn.sigmoid(z[:, d:])
    h2 = _rms(x1, gf_ref[...])
    hi = h2.astype(BF16)
    lo = (h2 - hi.astype(F32)).astype(BF16)
    logits = _dot(hi, wr_ref[0]) + (_dot(lo, wr_ref[0]) + _dot(hi, wr_ref[1]))
    idx, gate = _topk2(logits)
    return x1, h2, idx, gate


def _s5_glu_router(x, yf, yb, gm, dskip, wglu, gf, wr, tm):
    t, d = x.shape
    row = pl.BlockSpec((tm, d), lambda i: (i, 0))
    nar = pl.BlockSpec((tm, LANES), lambda i: (i, 0))
    return pl.pallas_call(
        _s5_glu_router_kernel, grid=(t // tm,),
        in_specs=[row, row, row, _resident(gm.shape), _resident(dskip.shape), _resident(wglu.shape),
                  _resident(gf.shape), _resident(wr.shape)],
        out_specs=[row, row, nar, nar],
        out_shape=[jax.ShapeDtypeStruct((t, d), F32), jax.ShapeDtypeStruct((t, d), F32),
                   jax.ShapeDtypeStruct((t, LANES), jnp.int32), jax.ShapeDtypeStruct((t, LANES), F32)],
        compiler_params=_cparams("parallel"), name="s5_glu_router",
    )(x, yf, yb, gm, dskip, wglu, gf, wr)


def _s5_scan(xp, g, bt, ct, lr, li, init, tt, reverse, emit_y):
    nrg, rows, d = xp.shape
    s = rows // SUBLANES
    nc = s // tt
    ntile, _, sw = bt.shape
    half = sw // 2
    blk = tt * SUBLANES
    if reverse:
        xmap = lambda gi, c: (gi, nc - 1 - c, 0)
    else:
        xmap = lambda gi, c: (gi, c, 0)
    st_shape = (ntile, 2, SUBLANES, half)
    fin_spec = pl.BlockSpec((1,) + st_shape, lambda gi, c: (gi, 0, 0, 0, 0))
    fin_shape = jax.ShapeDtypeStruct((nrg,) + st_shape, F32)
    if emit_y:
        out_specs = [pl.BlockSpec((1, blk, d), xmap), fin_spec]
        out_shape = [jax.ShapeDtypeStruct(xp.shape, F32), fin_shape]
    else:
        out_specs = [fin_spec]
        out_shape = [fin_shape]
    outs = pl.pallas_call(
        functools.partial(_s5_scan_kernel, tt=tt, reverse=reverse, emit_y=emit_y), grid=(nrg, nc),
        in_specs=[pl.BlockSpec((1, blk, d), xmap), _resident(g.shape), _resident(bt.shape),
                  _resident(ct.shape), _resident(lr.shape), _resident(li.shape),
                  pl.BlockSpec((1,) + st_shape, lambda gi, c: (gi, 0, 0, 0, 0))],
        out_specs=out_specs, out_shape=out_shape,
        scratch_shapes=[pltpu.VMEM((blk, sw), F32), pltpu.VMEM((blk, sw), F32), pltpu.VMEM(st_shape, F32)],
        compiler_params=_cparams("parallel", "arbitrary"),
        name="s5_scan_" + ("bwd" if reverse else "fwd") + ("" if emit_y else "_state"),
    )(xp, g, bt, ct, lr, li, init)
    if emit_y:
        return outs[0], outs[1]
    return None, outs[0]


def _topk2(logits):
    lane = lax.broadcasted_iota(jnp.int32, logits.shape, 1)
    lg = jnp.where(lane < N_EXPERTS, logits, -jnp.inf)
    m1 = jnp.max(lg, axis=-1, keepdims=True)
    i1 = jnp.min(jnp.where(lg == m1, lane, LANES), axis=-1, keepdims=True)
    lg2 = jnp.where(lane == i1, -jnp.inf, lg)
    m2 = jnp.max(lg2, axis=-1, keepdims=True)
    i2 = jnp.min(jnp.where(lg2 == m2, lane, LANES), axis=-1, keepdims=True)
    e = jnp.exp(m2 - m1)
    g1 = 1.0 / (1.0 + e)
    g2 = e / (1.0 + e)
    idx = jnp.where(lane == 0, i1, jnp.where(lane == 1, i2, 0))
    gate = jnp.where(lane == 0, g1, jnp.where(lane == 1, g2, 0.0))
    return idx, gate


def _moe_plan_kernel(idx_ref, ltri_ref, rank_ref, cnt_ref, carry_ref):
    @pl.when(pl.program_id(0) == 0)
    def _():
        carry_ref[...] = jnp.zeros_like(carry_ref)

    idx = idx_ref[...]
    lane = lax.broadcasted_iota(jnp.int32, idx.shape, 1)
    oh1 = lane == idx[:, 0:1]
    oh2 = lane == idx[:, 1:2]
    oh = jnp.where(oh1, 1.0, jnp.where(oh2, 1.0, 0.0)).astype(BF16)
    cs = _dot(ltri_ref[...], oh)
    tot = carry_ref[...] + cs - 1.0
    r1 = jnp.sum(jnp.where(oh1, tot, 0.0), axis=-1, keepdims=True)
    r2 = jnp.sum(jnp.where(oh2, tot, 0.0), axis=-1, keepdims=True)
    rank_ref[...] = jnp.where(lane == 0, r1, jnp.where(lane == 1, r2, 0.0)).astype(jnp.int32)
    new = carry_ref[...] + cs[cs.shape[0] - 1:, :]
    carry_ref[...] = new
    cnt_ref[...] = new


def _moe_plan(idx, tm):
    t = idx.shape[0]
    ltri = jnp.tril(jnp.ones((tm, tm), F32)).astype(BF16)
    nar = pl.BlockSpec((tm, LANES), lambda i: (i, 0))
    return pl.pallas_call(
        _moe_plan_kernel, grid=(t // tm,),
        in_specs=[nar, _resident(ltri.shape)],
        out_specs=[nar, pl.BlockSpec((1, LANES), lambda i: (0, 0))],
        out_shape=[jax.ShapeDtypeStruct((t, LANES), jnp.int32), jax.ShapeDtypeStruct((1, LANES), F32)],
        scratch_shapes=[pltpu.VMEM((1, LANES), F32)],
        compiler_params=_cparams("arbitrary"), name="moe_plan",
    )(idx, ltri)


ROW_UNROLL = 8


def _row_copy(src, dst, i, j, sem):
    return pltpu.make_async_copy(src.at[pl.ds(i, 1)], dst.at[pl.ds(j, 1)], sem)


def _moe_dispatch_kernel(pos_ref, x_ref, xs_in, xs_hbm, sem, *, tm):
    del xs_in

    def issue(g, c):
        for u in range(ROW_UNROLL):
            r = g * ROW_UNROLL + u
            _row_copy(x_ref, xs_hbm, r, pos_ref[0, 0, 2 * r], sem).start(priority=0)
            _row_copy(x_ref, xs_hbm, r, pos_ref[0, 0, 2 * r + 1], sem).start(priority=1)
        return c

    lax.fori_loop(0, tm // ROW_UNROLL, issue, 0)

    def drain(g, c):
        for _ in range(2 * ROW_UNROLL):
            _row_copy(x_ref, xs_hbm, 0, 0, sem).wait()
        return c

    lax.fori_loop(0, tm // ROW_UNROLL, drain, 0)


def _moe_dispatch(h2, pos, rows, tm):
    t, d = h2.shape
    pos3 = pos.reshape(t // tm, 1, 2 * tm)
    xs0 = jnp.zeros((rows, d), F32)
    return pl.pallas_call(
        functools.partial(_moe_dispatch_kernel, tm=tm), grid=(t // tm,),
        in_specs=[pl.BlockSpec((1, 1, 2 * tm), lambda i: (i, 0, 0), memory_space=pltpu.SMEM),
                  pl.BlockSpec((tm, d), lambda i: (i, 0)), pl.BlockSpec(memory_space=pl.ANY)],
        out_specs=pl.BlockSpec(memory_space=pl.ANY),
        out_shape=jax.ShapeDtypeStruct((rows, d), F32),
        scratch_shapes=[pltpu.SemaphoreType.DMA(())],
        input_output_aliases={2: 0},
        compiler_params=_cparams("arbitrary"), name="moe_dispatch",
    )(pos3, h2, xs0)


def _moe_expert_kernel(te_ref, nu_ref, xs_ref, wg_ref, wu_ref, wd_ref, y_ref, act_ref, *, fc):
    i = pl.program_id(0)

    @pl.when(i < nu_ref[0])
    def _():
        y_ref[...] = _swiglu(xs_ref[...].astype(BF16), wg_ref.at[0], wu_ref.at[0], wd_ref.at[0], act_ref, fc)

    @pl.when(i >= nu_ref[0])
    def _():
        y_ref[...] = jnp.zeros_like(y_ref)


def _moe_experts(xs, tile_expert, n_used, wg, wu, wd, tm):
    rows, d = xs.shape
    f = wg.shape[2]
    fc = _tile(f, 1792, MXU_DIM)
    wspec = lambda shp: pl.BlockSpec((1,) + shp, lambda i, te, nu: (te[i], 0, 0), pipeline_mode=pl.Buffered(1))
    gs = pltpu.PrefetchScalarGridSpec(
        num_scalar_prefetch=2, grid=(rows // tm,),
        in_specs=[pl.BlockSpec((tm, d), lambda i, te, nu: (i, 0)),
                  wspec((d, f)), wspec((d, f)), wspec((f, d))],
        out_specs=pl.BlockSpec((tm, d), lambda i, te, nu: (i, 0)),
        scratch_shapes=[pltpu.VMEM((tm, f), BF16)])
    return pl.pallas_call(
        functools.partial(_moe_expert_kernel, fc=fc), grid_spec=gs,
        out_shape=jax.ShapeDtypeStruct((rows, d), F32),
        compiler_params=_cparams("arbitrary"), name="moe_experts",
    )(tile_expert, n_used, xs, wg, wu, wd)


def _moe_combine_kernel(pos_ref, x_ref, gate_ref, ys_hbm, gf_ref, o_ref, b0_ref, b1_ref, sem, *, tm, final_norm):
    def issue(g, c):
        for u in range(ROW_UNROLL):
            r = g * ROW_UNROLL + u
            _row_copy(ys_hbm, b0_ref, pos_ref[0, 0, 2 * r], r, sem).start(priority=0)
            _row_copy(ys_hbm, b1_ref, pos_ref[0, 0, 2 * r + 1], r, sem).start(priority=1)
        return c

    lax.fori_loop(0, tm // ROW_UNROLL, issue, 0)

    def drain(g, c):
        for _ in range(ROW_UNROLL):
            _row_copy(ys_hbm, b0_ref, 0, 0, sem).wait()
            _row_copy(ys_hbm, b1_ref, 0, 0, sem).wait()
        return c

    lax.fori_loop(0, tm // ROW_UNROLL, drain, 0)
    gate = gate_ref[...]
    out = x_ref[...] + gate[:, 0:1] * b0_ref[...] + gate[:, 1:2] * b1_ref[...]
    if final_norm:
        out = _rms(out, gf_ref[...])
    o_ref[...] = out


def _moe_combine(x1, gate, ys, pos, gfin, tm, final_norm):
    t, d = x1.shape
    pos3 = pos.reshape(t // tm, 1, 2 * tm)
    return pl.pallas_call(
        functools.partial(_moe_combine_kernel, tm=tm, final_norm=final_norm), grid=(t // tm,),
        in_specs=[pl.BlockSpec((1, 1, 2 * tm), lambda i: (i, 0, 0), memory_space=pltpu.SMEM),
                  pl.BlockSpec((tm, d), lambda i: (i, 0)),
                  pl.BlockSpec((tm, LANES), lambda i: (i, 0)),
                  pl.BlockSpec(memory_space=pl.ANY), _resident(gfin.shape)],
        out_specs=pl.BlockSpec((tm, d), lambda i: (i, 0)),
        out_shape=jax.ShapeDtypeStruct((t, d), F32),
        scratch_shapes=[pltpu.VMEM((tm, d), F32), pltpu.VMEM((tm, d), F32), pltpu.SemaphoreType.DMA(())],
        compiler_params=_cparams("arbitrary"), name="moe_combine",
    )(pos3, x1, gate, ys, gfin)


def _rmsnorm_kernel(x_ref, g_ref, o_ref):
    o_ref[...] = _rms(x_ref[...], g_ref[...])


def _rmsnorm(x, g, tm):
    t, d = x.shape
    return pl.pallas_call(
        _rmsnorm_kernel, grid=(t // tm,),
        in_specs=[pl.BlockSpec((tm, d), lambda i: (i, 0)), _resident(g.shape)],
        out_specs=pl.BlockSpec((tm, d), lambda i: (i, 0)),
        out_shape=jax.ShapeDtypeStruct((t, d), F32),
        compiler_params=_cparams("parallel"), name="final_norm",
    )(x, g)


def _rot_cols(w):
    half = QK_ROPE // 2
    return jnp.concatenate([-w[..., half:], w[..., :half]], axis=-1)


def _prep_mla(w_dq, q_norm, w_uq, w_dkv, kv_norm, w_ukv, w_o):
    ql = w_dq.shape[1]
    kvl = w_ukv.shape[0]
    wq = w_uq.reshape(ql, N_HEADS, QK_NOPE + QK_ROPE)
    wq_nope, wq_rope = wq[..., :QK_NOPE], wq[..., QK_NOPE:]
    zpad = jnp.zeros((ql, N_HEADS, HEAD_PAD - QK_NOPE - QK_ROPE), F32)
    w1 = jnp.concatenate([wq_nope, wq_rope, zpad], axis=-1).reshape(ql, N_HEADS * HEAD_PAD)
    w2 = _rot_cols(wq_rope).reshape(ql, N_HEADS * QK_ROPE)
    kr_w = w_dkv[:, kvl:]
    lpad = jnp.zeros((w_dkv.shape[0], LANES - QK_ROPE), F32)
    wdkv = jnp.concatenate([w_dkv[:, :kvl], kr_w, lpad, _rot_cols(kr_w), lpad], axis=1)
    wkv = w_ukv.reshape(kvl, N_HEADS, QK_NOPE + V_DIM)
    wk_top = jnp.concatenate([wkv[..., :QK_NOPE], jnp.zeros((kvl, N_HEADS, HEAD_PAD - QK_NOPE), F32)], axis=-1)
    eye = jnp.eye(LANES, QK_ROPE, dtype=F32)
    wk_bot = jnp.concatenate([jnp.zeros((LANES, QK_NOPE), F32), eye,
                              jnp.zeros((LANES, HEAD_PAD - QK_NOPE - QK_ROPE), F32)], axis=-1)
    wk_bot = jnp.broadcast_to(wk_bot[:, None, :], (LANES, N_HEADS, HEAD_PAD))
    wk = jnp.concatenate([wk_top, wk_bot], axis=0).reshape(kvl + LANES, N_HEADS * HEAD_PAD)
    wvt = wkv[..., QK_NOPE:].reshape(kvl, N_HEADS * V_DIM).T
    return dict(wdq=w_dq.astype(BF16), qn=q_norm.reshape(1, -1), w1t=w1.T.astype(BF16), w2t=w2.T.astype(BF16),
                wdkv=wdkv.astype(BF16), kvn=kv_norm.reshape(1, -1), wk=wk.astype(BF16), wvt=wvt.astype(BF16),
                wo=w_o.astype(BF16))


def _rope_tables(s):
    pos = jnp.arange(s, dtype=F32)
    inv = ROPE_THETA ** (-jnp.arange(0, QK_ROPE, 2, dtype=F32) / QK_ROPE)
    ang = pos[:, None] * inv[None, :]
    ang = jnp.concatenate([ang, ang], axis=-1)
    cos, sin = jnp.cos(ang), jnp.sin(ang)
    qs = ATTN_SCALE * LOG2E
    zq = jnp.zeros((s, HEAD_PAD - QK_NOPE - QK_ROPE), F32)
    ctq = jnp.concatenate([jnp.full((s, QK_NOPE), qs, F32), cos * qs, zq], axis=1).T
    stq = (sin * qs).T
    zk = jnp.zeros((s, LANES - QK_ROPE), F32)
    ctk = jnp.concatenate([cos, zk], axis=1)
    stk = jnp.concatenate([sin, zk], axis=1)
    return ctq, stq, ctk, stk


def _prep_s5(a_re, a_im, log_step, b_re, b_im, c_re, c_im):
    _, g, p = a_re.shape
    n = g * p * GROUP_CH
    rep = lambda a: jnp.broadcast_to(a[..., None], (2, g, p, GROUP_CH)).reshape(2, n)
    ls = jnp.broadcast_to(log_step[:, :, None, None], (2, g, p, GROUP_CH)).reshape(2, n)
    outs = pl.pallas_call(
        _s5_param_kernel,
        out_shape=[jax.ShapeDtypeStruct((2, n), F32)] * 4, name="s5_params",
    )(rep(a_re), rep(a_im), ls, b_re.reshape(2, n), b_im.reshape(2, n))
    lr, li, bbr, bbi = [o.reshape(2, g, p, GROUP_CH) for o in outs]
    lr, li = lr[..., 0], li[..., 0]
    gpt = MXU_DIM // GROUP_CH
    ntile = g // gpt
    eye = jnp.eye(gpt, dtype=F32)

    def tiles_b(bb):
        t = bb.reshape(2, ntile, gpt, p, GROUP_CH)
        return jnp.einsum('djapc,ab->djacbp', t, eye).reshape(2, ntile, gpt * GROUP_CH, gpt * p)

    def tiles_c(cc):
        t = cc.reshape(2, ntile, gpt, GROUP_CH, p)
        return jnp.einsum('djacp,ab->djapbc', t, eye).reshape(2, ntile, gpt * p, gpt * GROUP_CH)

    bt = jnp.concatenate([tiles_b(bbr), tiles_b(bbi)], axis=-1).astype(BF16)
    ct = jnp.concatenate([tiles_c(c_re), tiles_c(-c_im)], axis=-2).astype(BF16)
    lrt = lr.reshape(2, ntile, 1, gpt * p)
    lit = li.reshape(2, ntile, 1, gpt * p)
    return bt, ct, lrt, lit


def _split_bf16(w):
    hi = w.astype(BF16)
    lo = (w - hi.astype(F32)).astype(BF16)
    return jnp.stack([hi, lo])


def _cpow(lr, li, n):
    k = int(math.log2(n))
    assert 2 ** k == n
    for _ in range(k):
        lr, li = lr * lr - li * li, 2.0 * lr * li
    return lr, li


def _mla_layer(x, p, j, tabs, tm, attn_cfg):
    w = p['mla'][j]
    qt, k, vt = _mla_pre(x, p['norm_mix'][2 * j], w, tabs, tm)
    ot = _attention(qt, k, vt, *attn_cfg)
    f = p['ffn'][j]
    return _attn_out_ffn(x, ot, w['wo'], p['norm_ffn'][2 * j], f['wg'], f['wu'], f['wd'], tm)


def _s5_states(xp, g, s5, nseg, tt):
    bt, ct, lr, li = s5
    nrg = xp.shape[0]
    ntile, half = bt.shape[1], bt.shape[3] // 2
    zero = jnp.zeros((nrg, ntile, 2, SUBLANES, half), F32)
    if nseg == 1:
        return zero, zero
    seg_len = xp.shape[1] // SUBLANES
    inits = []
    for k, reverse in ((0, False), (1, True)):
        _, fin = _s5_scan(xp, g, bt[k], ct[k], lr[k], li[k], zero, tt, reverse, False)
        e = fin.reshape(nrg, ntile, 2, SUBLANES // nseg, nseg, half)
        plr, pli = _cpow(lr[k], li[k], seg_len)
        plr, pli = plr[None], pli[None]
        order = range(nseg - 1, -1, -1) if reverse else range(nseg)
        sr = jnp.zeros_like(e[:, :, 0, :, 0])
        si = jnp.zeros_like(sr)
        init_r = [None] * nseg
        init_i = [None] * nseg
        for q in order:
            init_r[q], init_i[q] = sr, si
            er, ei = e[:, :, 0, :, q], e[:, :, 1, :, q]
            sr, si = plr * sr - pli * si + er, plr * si + pli * sr + ei
        init = jnp.stack([jnp.stack(init_r, axis=3), jnp.stack(init_i, axis=3)], axis=2)
        inits.append(init.reshape(nrg, ntile, 2, SUBLANES, half))
    return inits[0], inits[1]


def _s5_moe_layer(xp, p, j, nseg, tt, tm, last):
    nrg, rows, d = xp.shape
    i = 2 * j + 1
    s5 = p['s5'][j]
    bt, ct, lr, li = s5
    gm = p['norm_mix'][i]
    init_f, init_b = _s5_states(xp, gm, s5, nseg, tt)
    yf, _ = _s5_scan(xp, gm, bt[0], ct[0], lr[0], li[0], init_f, tt, False, True)
    yb, _ = _s5_scan(xp, gm, bt[1], ct[1], lr[1], li[1], init_b, tt, True, True)
    t = nrg * rows
    m = p['moe'][j]
    x1, h2, idx, gate = _s5_glu_router(xp.reshape(t, d), yf.reshape(t, d), yb.reshape(t, d), gm,
                                       p['s5_d'][j], p['s5_wglu'][j], p['norm_ffn'][i], m['wr'], tm)
    rank, cnt = _moe_plan(idx, tm)
    counts = cnt[0, :N_EXPERTS].astype(jnp.int32)
    ntiles = (counts + tm - 1) // tm
    tile_end = jnp.cumsum(ntiles)
    offs = (tile_end - ntiles) * tm
    e2 = idx[:, :2]
    pos = offs[e2] + rank[:, :2]
    nt_max = (2 * t) // tm + N_EXPERTS
    tile_expert = jnp.minimum(jnp.searchsorted(tile_end, jnp.arange(nt_max, dtype=jnp.int32), side='right'),
                              N_EXPERTS - 1).astype(jnp.int32)
    n_used = tile_end[-1:].astype(jnp.int32)
    xs = _moe_dispatch(h2, pos, nt_max * tm, tm)
    ys = _moe_experts(xs, tile_expert, n_used, m['wg'], m['wu'], m['wd'], tm)
    out = _moe_combine(x1, gate, ys, pos, p['norm_final'], tm, last)
    return out.reshape(nrg, rows, d)


def _to_perm(x, nseg):
    b, s, d = x.shape
    nrg = b * nseg // SUBLANES
    xr = x.reshape(nrg, SUBLANES, s // nseg, d)
    return jnp.swapaxes(xr, 1, 2).reshape(nrg, (s // nseg) * SUBLANES, d)


def _from_perm(xp, b, s, nseg):
    nrg, rows, d = xp.shape
    xr = xp.reshape(nrg, rows // SUBLANES, SUBLANES, d)
    return jnp.swapaxes(xr, 1, 2).reshape(b, s, d)


def _trunk(x, p):
    b, s, d = x.shape
    tm = _tile(s, 512)
    long_seq = s > 4096
    tq = _tile(s, 512 if long_seq else 2048, LANES)
    tk = _tile(s, min(512 if long_seq else 256, s // 4), LANES)
    attn_cfg = (tq, tk, 4 if long_seq else 2, 2)
    nseg = 1 if b % SUBLANES == 0 else SUBLANES // b
    assert (b * nseg) % SUBLANES == 0 and s % nseg == 0
    tt = _tile(s // nseg, 64)
    tabs = _rope_tables(s)
    depth = p['norm_mix'].shape[0]
    for i in range(depth):
        j = i // 2
        if i % 2 == 0:
            x = _mla_layer(x, p, j, tabs, tm, attn_cfg)
        else:
            xp = _s5_moe_layer(_to_perm(x, nseg), p, j, nseg, tt, tm, last=(i == depth - 1))
            x = _from_perm(xp, b, s, nseg)
    if depth % 2 == 1:
        x = _rmsnorm(x.reshape(b * s, d), p['norm_final'], tm).reshape(b, s, d)
    return x


def kernel(x_prompt, x_sample, norm_mix, norm_ffn, norm_final, mla_w_dq, mla_q_norm, mla_w_uq, mla_w_dkv,
           mla_kv_norm, mla_w_ukv, mla_w_o, ssm_a_re, ssm_a_im, ssm_log_step, ssm_b_re, ssm_b_im, ssm_c_re,
           ssm_c_im, ssm_d, ssm_w_glu, ffn_w_gate, ffn_w_up, ffn_w_down, moe_w_router, moe_w_gate, moe_w_up,
           moe_w_down):
    d = x_prompt.shape[-1]
    na, ns = mla_w_dq.shape[0], ssm_a_re.shape[0]
    p = dict(
        norm_mix=norm_mix.reshape(-1, 1, d), norm_ffn=norm_ffn.reshape(-1, 1, d), norm_final=norm_final.reshape(1, d),
        mla=[_prep_mla(mla_w_dq[j], mla_q_norm[j], mla_w_uq[j], mla_w_dkv[j], mla_kv_norm[j], mla_w_ukv[j],
                       mla_w_o[j]) for j in range(na)],
        ffn=[dict(wg=ffn_w_gate[j].astype(BF16), wu=ffn_w_up[j].astype(BF16), wd=ffn_w_down[j].astype(BF16))
             for j in range(na)],
        s5=[_prep_s5(ssm_a_re[j], ssm_a_im[j], ssm_log_step[j], ssm_b_re[j], ssm_b_im[j], ssm_c_re[j], ssm_c_im[j])
            for j in range(ns)],
        s5_d=[ssm_d[j].reshape(1, d) for j in range(ns)],
        s5_wglu=[ssm_w_glu[j].astype(BF16) for j in range(ns)],
        moe=[dict(wr=_split_bf16(jnp.pad(moe_w_router[j], ((0, 0), (0, LANES - N_EXPERTS)))),
                  wg=moe_w_gate[j].astype(BF16), wu=moe_w_up[j].astype(BF16), wd=moe_w_down[j].astype(BF16))
             for j in range(ns)],
    )
    return (_trunk(x_prompt, p), _trunk(x_sample, p))
```

```python
import functools
import math

import jax
import jax.numpy as jnp
from jax import lax
from jax.experimental import pallas as pl
from jax.experimental.pallas import tpu as pltpu

BF16 = jnp.bfloat16
F32 = jnp.float32

N_HEADS = 16
QK_NOPE = 64
QK_ROPE = 32
V_DIM = 64
HEAD_PAD = 128
ROPE_THETA = 10000.0
ATTN_SCALE = (QK_NOPE + QK_ROPE) ** -0.5
LOG2E = 1.4426950408889634
GROUP_CH = 16
STATE_DIM = 64
N_EXPERTS = 8
NORM_EPS = 1e-6
LANES = 128
SUBLANES = 8
MXU_DIM = 256
VMEM_LIMIT = 56 * 1024 * 1024
NEG_BIG = -1e30


def _cparams(*sem):
    return pltpu.CompilerParams(dimension_semantics=sem, vmem_limit_bytes=VMEM_LIMIT)


def _resident(shape):
    zeros = (0,) * len(shape)
    return pl.BlockSpec(shape, lambda *_: zeros, pipeline_mode=pl.Buffered(1))


def _tile(n, cap, mult=SUBLANES):
    if n <= cap:
        return n
    t = (cap // mult) * mult
    while t > mult and n % t:
        t -= mult
    assert n % t == 0, (n, cap, mult)
    return t


def _rms(x, g):
    ms = jnp.mean(x * x, axis=-1, keepdims=True)
    return x * lax.rsqrt(ms + NORM_EPS) * g


def _dot(a, b):
    return jnp.dot(a, b, preferred_element_type=F32)


def _dot_nt(a, b):
    return lax.dot_general(a, b, (((1,), (1,)), ((), ())), preferred_element_type=F32)


def _dot_tn(a, b):
    return lax.dot_general(a, b, (((0,), (0,)), ((), ())), preferred_element_type=F32)


def _swiglu(h, wg_ref, wu_ref, wd_ref, act_ref, fc):
    f = act_ref.shape[1]
    for c0 in range(0, f, fc):
        g = _dot(h, wg_ref[:, c0:c0 + fc])
        u = _dot(h, wu_ref[:, c0:c0 + fc])
        act_ref[:, c0:c0 + fc] = (g * jax.nn.sigmoid(g) * u).astype(BF16)
    return _dot(act_ref[...], wd_ref[...])


def _mla_pre_kernel(x_ref, g_ref, wdq_ref, qn_ref, w1_ref, w2_ref, wdkv_ref, kvn_ref, wk_ref, wv_ref,
                    ctq_ref, stq_ref, ctk_ref, stk_ref, qt_ref, k_ref, vt_ref):
    h = _rms(x_ref[0], g_ref[...]).astype(BF16)
    cq = _rms(_dot(h, wdq_ref[...]), qn_ref[...]).astype(BF16)
    a = _dot_nt(w1_ref[...], cq)
    b = _dot_nt(w2_ref[...], cq)
    ct = ctq_ref[...]
    st = stq_ref[...]
    r0, r1 = QK_NOPE, QK_NOPE + QK_ROPE
    for hh in range(N_HEADS):
        ah = a[hh * HEAD_PAD:(hh + 1) * HEAD_PAD] * ct
        rope = ah[r0:r1] + b[hh * QK_ROPE:(hh + 1) * QK_ROPE] * st
        qt_ref[0, hh * HEAD_PAD:(hh + 1) * HEAD_PAD, :] = jnp.concatenate([ah[:r0], rope, ah[r1:]], axis=0).astype(BF16)
    ck = _dot(h, wdkv_ref[...])
    kvl = kvn_ref.shape[1]
    ckv = _rms(ck[:, :kvl], kvn_ref[...]).astype(BF16)
    kr = (ck[:, kvl:kvl + LANES] * ctk_ref[...] + ck[:, kvl + LANES:] * stk_ref[...]).astype(BF16)
    kaug = jnp.concatenate([ckv, kr], axis=1)
    k_ref[0] = _dot(kaug, wk_ref[...]).astype(BF16)
    vt_ref[0] = _dot_nt(wv_ref[...], ckv).astype(BF16)


def _mla_pre(x, g, w, tabs, tm):
    b, s, d = x.shape
    hq = N_HEADS * HEAD_PAD
    hv = N_HEADS * V_DIM
    ctq, stq, ctk, stk = tabs
    grid = (b, s // tm)
    in_specs = [
        pl.BlockSpec((1, tm, d), lambda bi, i: (bi, i, 0)),
        _resident(g.shape), _resident(w['wdq'].shape), _resident(w['qn'].shape),
        _resident(w['w1t'].shape), _resident(w['w2t'].shape), _resident(w['wdkv'].shape),
        _resident(w['kvn'].shape), _resident(w['wk'].shape), _resident(w['wvt'].shape),
        pl.BlockSpec((HEAD_PAD, tm), lambda bi, i: (0, i)),
        pl.BlockSpec((QK_ROPE, tm), lambda bi, i: (0, i)),
        pl.BlockSpec((tm, LANES), lambda bi, i: (i, 0)),
        pl.BlockSpec((tm, LANES), lambda bi, i: (i, 0)),
    ]
    out_specs = [
        pl.BlockSpec((1, hq, tm), lambda bi, i: (bi, 0, i)),
        pl.BlockSpec((1, tm, hq), lambda bi, i: (bi, i, 0)),
        pl.BlockSpec((1, hv, tm), lambda bi, i: (bi, 0, i)),
    ]
    out_shape = [jax.ShapeDtypeStruct((b, hq, s), BF16), jax.ShapeDtypeStruct((b, s, hq), BF16),
                 jax.ShapeDtypeStruct((b, hv, s), BF16)]
    return pl.pallas_call(
        _mla_pre_kernel, grid=grid, in_specs=in_specs, out_specs=out_specs, out_shape=out_shape,
        compiler_params=_cparams("parallel", "parallel"), name="mla_pre",
    )(x, g, w['wdq'], w['qn'], w['w1t'], w['w2t'], w['wdkv'], w['kvn'], w['wk'], w['wvt'], ctq, stq, ctk, stk)


SUM_ROWS = 16


def _attn_kernel(qt_ref, k_ref, vt_ref, o_ref, *rest, tk, unroll, nq):
    per = len(rest) // nq
    tq = qt_ref.shape[2] // nq
    for qi in range(nq):
        _attn_tile(qt_ref[0, :, qi * tq:(qi + 1) * tq], k_ref, vt_ref, o_ref, slice(qi * tq, (qi + 1) * tq),
                   rest[qi * per:(qi + 1) * per - 2], rest[(qi + 1) * per - 2], rest[(qi + 1) * per - 1], tk, unroll)


def _attn_tile(qt, k_ref, vt_ref, o_ref, lanes, s_refs, acc_ref, m_ref, tk, unroll):
    nbuf = len(s_refs)
    nk = k_ref.shape[1] // tk
    ones = jnp.ones((SUM_ROWS, tk), BF16)

    def scores(j, dst_ref):
        k0 = pl.multiple_of(j * tk, tk)
        dst_ref[...] = _dot(k_ref[0, pl.ds(k0, tk), :], qt)

    def update(j, src_ref):
        k0 = pl.multiple_of(j * tk, tk)
        s = src_ref[...]
        m_old = m_ref[...]
        m_new = jnp.maximum(m_old, jnp.max(s, axis=0, keepdims=True))
        p = jnp.exp2(s - m_new).astype(BF16)
        v = jnp.concatenate([vt_ref[0, :, pl.ds(k0, tk)], ones], axis=0)
        acc_ref[...] = jnp.exp2(m_old - m_new) * acc_ref[...] + _dot(v, p)
        m_ref[...] = m_new

    m_ref[...] = jnp.full(m_ref.shape, NEG_BIG, F32)
    acc_ref[...] = jnp.zeros(acc_ref.shape, F32)
    ahead = nbuf - 1
    for b in range(ahead):
        scores(b, s_refs[b])
    n_full = (nk - 1 - 2 * ahead) // nbuf + 1 if nk > 2 * ahead else 0

    def ring(t, c):
        j = t * nbuf
        for u in range(nbuf):
            scores(j + u + ahead, s_refs[(u + ahead) % nbuf])
            update(j + u, s_refs[u])
        return c

    lax.fori_loop(0, n_full, ring, 0, unroll=unroll)
    for b in range(n_full * nbuf, nk):
        if b + ahead < nk:
            scores(b + ahead, s_refs[(b + ahead) % nbuf])
        update(b, s_refs[b % nbuf])
    acc = acc_ref[...]
    o_ref[0, :, lanes] = (acc[:V_DIM] / acc[V_DIM:V_DIM + 1]).astype(BF16)


def _attention(qt, k, vt, tq, tk, unroll, nbuf, nq):
    b, hq, s = qt.shape
    assert s // tk >= nbuf - 1 and s % (nq * tq) == 0
    grid = (b, N_HEADS, s // (nq * tq))
    return pl.pallas_call(
        functools.partial(_attn_kernel, tk=tk, unroll=unroll, nq=nq), grid=grid,
        in_specs=[pl.BlockSpec((1, HEAD_PAD, nq * tq), lambda bi, h, i: (bi, h, i)),
                  pl.BlockSpec((1, s, HEAD_PAD), lambda bi, h, i: (bi, 0, h)),
                  pl.BlockSpec((1, V_DIM, s), lambda bi, h, i: (bi, h, 0))],
        out_specs=pl.BlockSpec((1, V_DIM, nq * tq), lambda bi, h, i: (bi, h, i)),
        out_shape=jax.ShapeDtypeStruct((b, N_HEADS * V_DIM, s), BF16),
        scratch_shapes=([pltpu.VMEM((tk, tq), F32)] * nbuf + [
            pltpu.VMEM((V_DIM + SUM_ROWS, tq), F32), pltpu.VMEM((1, tq), F32)]) * nq,
        compiler_params=_cparams("parallel", "parallel", "parallel"), name="mla_attention",
    )(qt, k, vt)


def _attn_out_ffn_kernel(x_ref, ot_ref, wo_ref, g_ref, wg_ref, wu_ref, wd_ref, y_ref, act_ref, *, fc):
    x1 = x_ref[0] + _dot_tn(ot_ref[0], wo_ref[...])
    h = _rms(x1, g_ref[...]).astype(BF16)
    y_ref[0] = x1 + _swiglu(h, wg_ref, wu_ref, wd_ref, act_ref, fc)


def _attn_out_ffn(x, ot, wo, g, wg, wu, wd, tm):
    b, s, d = x.shape
    f = wg.shape[1]
    fc = f
    hv = ot.shape[1]
    return pl.pallas_call(
        functools.partial(_attn_out_ffn_kernel, fc=fc), grid=(b, s // tm),
        in_specs=[pl.BlockSpec((1, tm, d), lambda bi, i: (bi, i, 0)),
                  pl.BlockSpec((1, hv, tm), lambda bi, i: (bi, 0, i)),
                  _resident(wo.shape), _resident(g.shape), _resident(wg.shape), _resident(wu.shape),
                  _resident(wd.shape)],
        out_specs=pl.BlockSpec((1, tm, d), lambda bi, i: (bi, i, 0)),
        out_shape=jax.ShapeDtypeStruct((b, s, d), F32),
        scratch_shapes=[pltpu.VMEM((tm, f), BF16)],
        compiler_params=_cparams("parallel", "parallel"), name="attn_out_ffn",
    )(x, ot, wo, g, wg, wu, wd)


def _s5_param_kernel(are_ref, aim_ref, ls_ref, bre_ref, bim_ref, lr_ref, li_ref, bbr_ref, bbi_ref):
    ar = are_ref[...]
    ai = aim_ref[...]
    delta = jnp.exp(ls_ref[...])
    mag = jnp.exp(ar * delta)
    lr = mag * jnp.cos(ai * delta)
    li = mag * jnp.sin(ai * delta)
    nr = lr - 1.0
    den = ar * ar + ai * ai
    cr = (nr * ar + li * ai) / den
    ci = (li * ar - nr * ai) / den
    br = bre_ref[...]
    bi = bim_ref[...]
    lr_ref[...] = lr
    li_ref[...] = li
    bbr_ref[...] = cr * br - ci * bi
    bbi_ref[...] = cr * bi + ci * br


def _s5_scan_kernel(x_ref, g_ref, bt_ref, ct_ref, lr_ref, li_ref, init_ref, *rest, tt, reverse, emit_y):
    if emit_y:
        y_ref, fin_ref, bu0_ref, bu1_ref, st_ref = rest
    else:
        fin_ref, bu0_ref, bu1_ref, st_ref = rest
    c = pl.program_id(1)

    @pl.when(c == 0)
    def _():
        st_ref[...] = init_ref[0]

    hb = _rms(x_ref[0], g_ref[...]).astype(BF16)

    def write_y(j, y):
        y_ref[0, :, j * MXU_DIM:(j + 1) * MXU_DIM] = y

    _s5_direction(hb, bt_ref, ct_ref, lr_ref, li_ref, st_ref, (bu0_ref, bu1_ref), tt, reverse,
                  write_y if emit_y else None)

    @pl.when(c == pl.num_programs(1) - 1)
    def _():
        fin_ref[0] = st_ref[...]


def _s5_direction(hb, bt_ref, ct_ref, lr_ref, li_ref, st_ref, bu_refs, tt, reverse, write_y):
    for j in range(bt_ref.shape[0]):
        _s5_tile(hb, j, bt_ref, ct_ref, lr_ref, li_ref, st_ref, bu_refs[j % 2], tt, reverse, write_y)


def _s5_tile(hb, j, bt_ref, ct_ref, lr_ref, li_ref, st_ref, bu_ref, tt, reverse, write_y):
    half = bt_ref.shape[2] // 2
    bu_ref[...] = _dot(hb[:, j * MXU_DIM:(j + 1) * MXU_DIM], bt_ref[j])
    lr = jnp.broadcast_to(lr_ref[j], (SUBLANES, half))
    li = jnp.broadcast_to(li_ref[j], (SUBLANES, half))
    xr, xi = st_ref[j, 0], st_ref[j, 1]
    for t in range(tt):
        r0 = ((tt - 1 - t) if reverse else t) * SUBLANES
        nr = lr * xr - li * xi + bu_ref[r0:r0 + SUBLANES, 0:half]
        ni = lr * xi + li * xr + bu_ref[r0:r0 + SUBLANES, half:2 * half]
        bu_ref[r0:r0 + SUBLANES, 0:half] = nr
        bu_ref[r0:r0 + SUBLANES, half:2 * half] = ni
        xr, xi = nr, ni
    st_ref[j, 0] = xr
    st_ref[j, 1] = xi
    if write_y is not None:
        write_y(j, _dot(bu_ref[...].astype(BF16), ct_ref[j]))


def _s5_glu_router_kernel(x_ref, yf_ref, yb_ref, gm_ref, d_ref, wglu_ref, gf_ref, wr_ref,
                          x1_ref, h2_ref, idx_ref, gate_ref):
    tm = x_ref.shape[0]
    sub = tm // 2 if tm % (2 * SUBLANES) == 0 else tm
    for r0 in range(0, tm, sub):
        rows = slice(r0, r0 + sub)
        x1, h2, idx, gate = _glu_router_rows(x_ref[rows, :], yf_ref[rows, :] + yb_ref[rows, :], gm_ref, d_ref,
                                             wglu_ref, gf_ref, wr_ref)
        x1_ref[rows, :] = x1
        h2_ref[rows, :] = h2
        idx_ref[rows, :] = idx
        gate_ref[rows, :] = gate


def _glu_router_rows(x, ysum, gm_ref, d_ref, wglu_ref, gf_ref, wr_ref):
    d = x.shape[1]
    y = ysum + d_ref[...] * _rms(x, gm_ref[...])
    g = jax.nn.gelu(y, approximate=True).astype(BF16)
    z = _dot(g, wglu_ref[...])
    x1 = x + z[:, :d] * jax.nn.sigmoid(z[:, d:])
    h2 = _rms(x1, gf_ref[...])
    hi = h2.astype(BF16)
    lo = (h2 - hi.astype(F32)).astype(BF16)
    logits = _dot(hi, wr_ref[0]) + (_dot(lo, wr_ref[0]) + _dot(hi, wr_ref[1]))
    idx, gate = _topk2(logits)
    return x1, h2, idx, gate


def _s5_glu_router(x, yf, yb, gm, dskip, wglu, gf, wr, tm):
    t, d = x.shape
    row = pl.BlockSpec((tm, d), lambda i: (i, 0))
    nar = pl.BlockSpec((tm, LANES), lambda i: (i, 0))
    return pl.pallas_call(
        _s5_glu_router_kernel, grid=(t // tm,),
        in_specs=[row, row, row, _resident(gm.shape), _resident(dskip.shape), _resident(wglu.shape),
                  _resident(gf.shape), _resident(wr.shape)],
        out_specs=[row, row, nar, nar],
        out_shape=[jax.ShapeDtypeStruct((t, d), F32), jax.ShapeDtypeStruct((t, d), F32),
                   jax.ShapeDtypeStruct((t, LANES), jnp.int32), jax.ShapeDtypeStruct((t, LANES), F32)],
        compiler_params=_cparams("parallel"), name="s5_glu_router",
    )(x, yf, yb, gm, dskip, wglu, gf, wr)


def _s5_scan(xp, g, bt, ct, lr, li, init, tt, reverse, emit_y):
    nrg, rows, d = xp.shape
    s = rows // SUBLANES
    nc = s // tt
    ntile, _, sw = bt.shape
    half = sw // 2
    blk = tt * SUBLANES
    if reverse:
        xmap = lambda gi, c: (gi, nc - 1 - c, 0)
    else:
        xmap = lambda gi, c: (gi, c, 0)
    st_shape = (ntile, 2, SUBLANES, half)
    fin_spec = pl.BlockSpec((1,) + st_shape, lambda gi, c: (gi, 0, 0, 0, 0))
    fin_shape = jax.ShapeDtypeStruct((nrg,) + st_shape, F32)
    if emit_y:
        out_specs = [pl.BlockSpec((1, blk, d), xmap), fin_spec]
        out_shape = [jax.ShapeDtypeStruct(xp.shape, F32), fin_shape]
    else:
        out_specs = [fin_spec]
        out_shape = [fin_shape]
    outs = pl.pallas_call(
        functools.partial(_s5_scan_kernel, tt=tt, reverse=reverse, emit_y=emit_y), grid=(nrg, nc),
        in_specs=[pl.BlockSpec((1, blk, d), xmap), _resident(g.shape), _resident(bt.shape),
                  _resident(ct.shape), _resident(lr.shape), _resident(li.shape),
                  pl.BlockSpec((1,) + st_shape, lambda gi, c: (gi, 0, 0, 0, 0))],
        out_specs=out_specs, out_shape=out_shape,
        scratch_shapes=[pltpu.VMEM((blk, sw), F32), pltpu.VMEM((blk, sw), F32), pltpu.VMEM(st_shape, F32)],
        compiler_params=_cparams("parallel", "arbitrary"),
        name="s5_scan_" + ("bwd" if reverse else "fwd") + ("" if emit_y else "_state"),
    )(xp, g, bt, ct, lr, li, init)
    if emit_y:
        return outs[0], outs[1]
    return None, outs[0]


def _topk2(logits):
    lane = lax.broadcasted_iota(jnp.int32, logits.shape, 1)
    lg = jnp.where(lane < N_EXPERTS, logits, -jnp.inf)
    m1 = jnp.max(lg, axis=-1, keepdims=True)
    i1 = jnp.min(jnp.where(lg == m1, lane, LANES), axis=-1, keepdims=True)
    lg2 = jnp.where(lane == i1, -jnp.inf, lg)
    m2 = jnp.max(lg2, axis=-1, keepdims=True)
    i2 = jnp.min(jnp.where(lg2 == m2, lane, LANES), axis=-1, keepdims=True)
    e = jnp.exp(m2 - m1)
    g1 = 1.0 / (1.0 + e)
    g2 = e / (1.0 + e)
    idx = jnp.where(lane == 0, i1, jnp.where(lane == 1, i2, 0))
    gate = jnp.where(lane == 0, g1, jnp.where(lane == 1, g2, 0.0))
    return idx, gate


def _moe_plan_kernel(idx_ref, ltri_ref, rank_ref, cnt_ref, carry_ref):
    @pl.when(pl.program_id(0) == 0)
    def _():
        carry_ref[...] = jnp.zeros_like(carry_ref)

    idx = idx_ref[...]
    lane = lax.broadcasted_iota(jnp.int32, idx.shape, 1)
    oh1 = lane == idx[:, 0:1]
    oh2 = lane == idx[:, 1:2]
    oh = jnp.where(oh1, 1.0, jnp.where(oh2, 1.0, 0.0)).astype(BF16)
    cs = _dot(ltri_ref[...], oh)
    tot = carry_ref[...] + cs - 1.0
    r1 = jnp.sum(jnp.where(oh1, tot, 0.0), axis=-1, keepdims=True)
    r2 = jnp.sum(jnp.where(oh2, tot, 0.0), axis=-1, keepdims=True)
    rank_ref[...] = jnp.where(lane == 0, r1, jnp.where(lane == 1, r2, 0.0)).astype(jnp.int32)
    new = carry_ref[...] + cs[cs.shape[0] - 1:, :]
    carry_ref[...] = new
    cnt_ref[...] = new


def _moe_plan(idx, tm):
    t = idx.shape[0]
    ltri = jnp.tril(jnp.ones((tm, tm), F32)).astype(BF16)
    nar = pl.BlockSpec((tm, LANES), lambda i: (i, 0))
    return pl.pallas_call(
        _moe_plan_kernel, grid=(t // tm,),
        in_specs=[nar, _resident(ltri.shape)],
        out_specs=[nar, pl.BlockSpec((1, LANES), lambda i: (0, 0))],
        out_shape=[jax.ShapeDtypeStruct((t, LANES), jnp.int32), jax.ShapeDtypeStruct((1, LANES), F32)],
        scratch_shapes=[pltpu.VMEM((1, LANES), F32)],
        compiler_params=_cparams("arbitrary"), name="moe_plan",
    )(idx, ltri)


ROW_UNROLL = 8


def _row_copy(src, dst, i, j, sem):
    return pltpu.make_async_copy(src.at[pl.ds(i, 1)], dst.at[pl.ds(j, 1)], sem)


def _moe_dispatch_kernel(pos_ref, x_ref, xs_in, xs_hbm, sem, *, tm):
    del xs_in

    def issue(g, c):
        for u in range(ROW_UNROLL):
            r = g * ROW_UNROLL + u
            _row_copy(x_ref, xs_hbm, r, pos_ref[0, 0, 2 * r], sem).start(priority=0)
            _row_copy(x_ref, xs_hbm, r, pos_ref[0, 0, 2 * r + 1], sem).start(priority=1)
        return c

    lax.fori_loop(0, tm // ROW_UNROLL, issue, 0)

    def drain(g, c):
        for _ in range(2 * ROW_UNROLL):
            _row_copy(x_ref, xs_hbm, 0, 0, sem).wait()
        return c

    lax.fori_loop(0, tm // ROW_UNROLL, drain, 0)


def _moe_dispatch(h2, pos, rows, tm):
    t, d = h2.shape
    pos3 = pos.reshape(t // tm, 1, 2 * tm)
    xs0 = jnp.zeros((rows, d), F32)
    return pl.pallas_call(
        functools.partial(_moe_dispatch_kernel, tm=tm), grid=(t // tm,),
        in_specs=[pl.BlockSpec((1, 1, 2 * tm), lambda i: (i, 0, 0), memory_space=pltpu.SMEM),
                  pl.BlockSpec((tm, d), lambda i: (i, 0)), pl.BlockSpec(memory_space=pl.ANY)],
        out_specs=pl.BlockSpec(memory_space=pl.ANY),
        out_shape=jax.ShapeDtypeStruct((rows, d), F32),
        scratch_shapes=[pltpu.SemaphoreType.DMA(())],
        input_output_aliases={2: 0},
        compiler_params=_cparams("arbitrary"), name="moe_dispatch",
    )(pos3, h2, xs0)


def _moe_expert_kernel(te_ref, nu_ref, xs_ref, wg_ref, wu_ref, wd_ref, y_ref, act_ref, *, fc):
    i = pl.program_id(0)

    @pl.when(i < nu_ref[0])
    def _():
        y_ref[...] = _swiglu(xs_ref[...].astype(BF16), wg_ref.at[0], wu_ref.at[0], wd_ref.at[0], act_ref, fc)

    @pl.when(i >= nu_ref[0])
    def _():
        y_ref[...] = jnp.zeros_like(y_ref)


def _moe_experts(xs, tile_expert, n_used, wg, wu, wd, tm):
    rows, d = xs.shape
    f = wg.shape[2]
    fc = _tile(f, 1792, MXU_DIM)
    wspec = lambda shp: pl.BlockSpec((1,) + shp, lambda i, te, nu: (te[i], 0, 0), pipeline_mode=pl.Buffered(1))
    gs = pltpu.PrefetchScalarGridSpec(
        num_scalar_prefetch=2, grid=(rows // tm,),
        in_specs=[pl.BlockSpec((tm, d), lambda i, te, nu: (i, 0)),
                  wspec((d, f)), wspec((d, f)), wspec((f, d))],
        out_specs=pl.BlockSpec((tm, d), lambda i, te, nu: (i, 0)),
        scratch_shapes=[pltpu.VMEM((tm, f), BF16)])
    return pl.pallas_call(
        functools.partial(_moe_expert_kernel, fc=fc), grid_spec=gs,
        out_shape=jax.ShapeDtypeStruct((rows, d), F32),
        compiler_params=_cparams("arbitrary"), name="moe_experts",
    )(tile_expert, n_used, xs, wg, wu, wd)


def _moe_combine_kernel(pos_ref, x_ref, gate_ref, ys_hbm, gf_ref, o_ref, b0_ref, b1_ref, sem, *, tm, final_norm):
    def issue(g, c):
        for u in range(ROW_UNROLL):
            r = g * ROW_UNROLL + u
            _row_copy(ys_hbm, b0_ref, pos_ref[0, 0, 2 * r], r, sem).start(priority=0)
            _row_copy(ys_hbm, b1_ref, pos_ref[0, 0, 2 * r + 1], r, sem).start(priority=1)
        return c

    lax.fori_loop(0, tm // ROW_UNROLL, issue, 0)

    def drain(g, c):
        for _ in range(ROW_UNROLL):
            _row_copy(ys_hbm, b0_ref, 0, 0, sem).wait()
            _row_copy(ys_hbm, b1_ref, 0, 0, sem).wait()
        return c

    lax.fori_loop(0, tm // ROW_UNROLL, drain, 0)
    gate = gate_ref[...]
    out = x_ref[...] + gate[:, 0:1] * b0_ref[...] + gate[:, 1:2] * b1_ref[...]
    if final_norm:
        out = _rms(out, gf_ref[...])
    o_ref[...] = out


def _moe_combine(x1, gate, ys, pos, gfin, tm, final_norm):
    t, d = x1.shape
    pos3 = pos.reshape(t // tm, 1, 2 * tm)
    return pl.pallas_call(
        functools.partial(_moe_combine_kernel, tm=tm, final_norm=final_norm), grid=(t // tm,),
        in_specs=[pl.BlockSpec((1, 1, 2 * tm), lambda i: (i, 0, 0), memory_space=pltpu.SMEM),
                  pl.BlockSpec((tm, d), lambda i: (i, 0)),
                  pl.BlockSpec((tm, LANES), lambda i: (i, 0)),
                  pl.BlockSpec(memory_space=pl.ANY), _resident(gfin.shape)],
        out_specs=pl.BlockSpec((tm, d), lambda i: (i, 0)),
        out_shape=jax.ShapeDtypeStruct((t, d), F32),
        scratch_shapes=[pltpu.VMEM((tm, d), F32), pltpu.VMEM((tm, d), F32), pltpu.SemaphoreType.DMA(())],
        compiler_params=_cparams("arbitrary"), name="moe_combine",
    )(pos3, x1, gate, ys, gfin)


def _rmsnorm_kernel(x_ref, g_ref, o_ref):
    o_ref[...] = _rms(x_ref[...], g_ref[...])


def _rmsnorm(x, g, tm):
    t, d = x.shape
    return pl.pallas_call(
        _rmsnorm_kernel, grid=(t // tm,),
        in_specs=[pl.BlockSpec((tm, d), lambda i: (i, 0)), _resident(g.shape)],
        out_specs=pl.BlockSpec((tm, d), lambda i: (i, 0)),
        out_shape=jax.ShapeDtypeStruct((t, d), F32),
        compiler_params=_cparams("parallel"), name="final_norm",
    )(x, g)


def _rot_cols(w):
    half = QK_ROPE // 2
    return jnp.concatenate([-w[..., half:], w[..., :half]], axis=-1)


def _prep_mla(w_dq, q_norm, w_uq, w_dkv, kv_norm, w_ukv, w_o):
    ql = w_dq.shape[1]
    kvl = w_ukv.shape[0]
    wq = w_uq.reshape(ql, N_HEADS, QK_NOPE + QK_ROPE)
    wq_nope, wq_rope = wq[..., :QK_NOPE], wq[..., QK_NOPE:]
    zpad = jnp.zeros((ql, N_HEADS, HEAD_PAD - QK_NOPE - QK_ROPE), F32)
    w1 = jnp.concatenate([wq_nope, wq_rope, zpad], axis=-1).reshape(ql, N_HEADS * HEAD_PAD)
    w2 = _rot_cols(wq_rope).reshape(ql, N_HEADS * QK_ROPE)
    kr_w = w_dkv[:, kvl:]
    lpad = jnp.zeros((w_dkv.shape[0], LANES - QK_ROPE), F32)
    wdkv = jnp.concatenate([w_dkv[:, :kvl], kr_w, lpad, _rot_cols(kr_w), lpad], axis=1)
    wkv = w_ukv.reshape(kvl, N_HEADS, QK_NOPE + V_DIM)
    wk_top = jnp.concatenate([wkv[..., :QK_NOPE], jnp.zeros((kvl, N_HEADS, HEAD_PAD - QK_NOPE), F32)], axis=-1)
    eye = jnp.eye(LANES, QK_ROPE, dtype=F32)
    wk_bot = jnp.concatenate([jnp.zeros((LANES, QK_NOPE), F32), eye,
                              jnp.zeros((LANES, HEAD_PAD - QK_NOPE - QK_ROPE), F32)], axis=-1)
    wk_bot = jnp.broadcast_to(wk_bot[:, None, :], (LANES, N_HEADS, HEAD_PAD))
    wk = jnp.concatenate([wk_top, wk_bot], axis=0).reshape(kvl + LANES, N_HEADS * HEAD_PAD)
    wvt = wkv[..., QK_NOPE:].reshape(kvl, N_HEADS * V_DIM).T
    return dict(wdq=w_dq.astype(BF16), qn=q_norm.reshape(1, -1), w1t=w1.T.astype(BF16), w2t=w2.T.astype(BF16),
                wdkv=wdkv.astype(BF16), kvn=kv_norm.reshape(1, -1), wk=wk.astype(BF16), wvt=wvt.astype(BF16),
                wo=w_o.astype(BF16))


def _rope_tables(s):
    pos = jnp.arange(s, dtype=F32)
    inv = ROPE_THETA ** (-jnp.arange(0, QK_ROPE, 2, dtype=F32) / QK_ROPE)
    ang = pos[:, None] * inv[None, :]
    ang = jnp.concatenate([ang, ang], axis=-1)
    cos, sin = jnp.cos(ang), jnp.sin(ang)
    qs = ATTN_SCALE * LOG2E
    zq = jnp.zeros((s, HEAD_PAD - QK_NOPE - QK_ROPE), F32)
    ctq = jnp.concatenate([jnp.full((s, QK_NOPE), qs, F32), cos * qs, zq], axis=1).T
    stq = (sin * qs).T
    zk = jnp.zeros((s, LANES - QK_ROPE), F32)
    ctk = jnp.concatenate([cos, zk], axis=1)
    stk = jnp.concatenate([sin, zk], axis=1)
    return ctq, stq, ctk, stk


def _prep_s5(a_re, a_im, log_step, b_re, b_im, c_re, c_im):
    _, g, p = a_re.shape
    n = g * p * GROUP_CH
    rep = lambda a: jnp.broadcast_to(a[..., None], (2, g, p, GROUP_CH)).reshape(2, n)
    ls = jnp.broadcast_to(log_step[:, :, None, None], (2, g, p, GROUP_CH)).reshape(2, n)
    outs = pl.pallas_call(
        _s5_param_kernel,
        out_shape=[jax.ShapeDtypeStruct((2, n), F32)] * 4, name="s5_params",
    )(rep(a_re), rep(a_im), ls, b_re.reshape(2, n), b_im.reshape(2, n))
    lr, li, bbr, bbi = [o.reshape(2, g, p, GROUP_CH) for o in outs]
    lr, li = lr[..., 0], li[..., 0]
    gpt = MXU_DIM // GROUP_CH
    ntile = g // gpt
    eye = jnp.eye(gpt, dtype=F32)

    def tiles_b(bb):
        t = bb.reshape(2, ntile, gpt, p, GROUP_CH)
        return jnp.einsum('djapc,ab->djacbp', t, eye).reshape(2, ntile, gpt * GROUP_CH, gpt * p)

    def tiles_c(cc):
        t = cc.reshape(2, ntile, gpt, GROUP_CH, p)
        return jnp.einsum('djacp,ab->djapbc', t, eye).reshape(2, ntile, gpt * p, gpt * GROUP_CH)

    bt = jnp.concatenate([tiles_b(bbr), tiles_b(bbi)], axis=-1).astype(BF16)
    ct = jnp.concatenate([tiles_c(c_re), tiles_c(-c_im)], axis=-2).astype(BF16)
    lrt = lr.reshape(2, ntile, 1, gpt * p)
    lit = li.reshape(2, ntile, 1, gpt * p)
    return bt, ct, lrt, lit


def _split_bf16(w):
    hi = w.astype(BF16)
    lo = (w - hi.astype(F32)).astype(BF16)
    return jnp.stack([hi, lo])


def _cpow(lr, li, n):
    k = int(math.log2(n))
    assert 2 ** k == n
    for _ in range(k):
        lr, li = lr * lr - li * li, 2.0 * lr * li
    return lr, li


def _mla_layer(x, p, j, tabs, tm, attn_cfg):
    w = p['mla'][j]
    qt, k, vt = _mla_pre(x, p['norm_mix'][2 * j], w, tabs, tm)
    ot = _attention(qt, k, vt, *attn_cfg)
    f = p['ffn'][j]
    return _attn_out_ffn(x, ot, w['wo'], p['norm_ffn'][2 * j], f['wg'], f['wu'], f['wd'], tm)


def _s5_states(xp, g, s5, nseg, tt):
    bt, ct, lr, li = s5
    nrg = xp.shape[0]
    ntile, half = bt.shape[1], bt.shape[3] // 2
    zero = jnp.zeros((nrg, ntile, 2, SUBLANES, half), F32)
    if nseg == 1:
        return zero, zero
    seg_len = xp.shape[1] // SUBLANES
    inits = []
    for k, reverse in ((0, False), (1, True)):
        _, fin = _s5_scan(xp, g, bt[k], ct[k], lr[k], li[k], zero, tt, reverse, False)
        e = fin.reshape(nrg, ntile, 2, SUBLANES // nseg, nseg, half)
        plr, pli = _cpow(lr[k], li[k], seg_len)
        plr, pli = plr[None], pli[None]
        order = range(nseg - 1, -1, -1) if reverse else range(nseg)
        sr = jnp.zeros_like(e[:, :, 0, :, 0])
        si = jnp.zeros_like(sr)
        init_r = [None] * nseg
        init_i = [None] * nseg
        for q in order:
            init_r[q], init_i[q] = sr, si
            er, ei = e[:, :, 0, :, q], e[:, :, 1, :, q]
            sr, si = plr * sr - pli * si + er, plr * si + pli * sr + ei
        init = jnp.stack([jnp.stack(init_r, axis=3), jnp.stack(init_i, axis=3)], axis=2)
        inits.append(init.reshape(nrg, ntile, 2, SUBLANES, half))
    return inits[0], inits[1]


def _s5_moe_layer(xp, p, j, nseg, tt, tm, last):
    nrg, rows, d = xp.shape
    i = 2 * j + 1
    s5 = p['s5'][j]
    bt, ct, lr, li = s5
    gm = p['norm_mix'][i]
    init_f, init_b = _s5_states(xp, gm, s5, nseg, tt)
    yf, _ = _s5_scan(xp, gm, bt[0], ct[0], lr[0], li[0], init_f, tt, False, True)
    yb, _ = _s5_scan(xp, gm, bt[1], ct[1], lr[1], li[1], init_b, tt, True, True)
    t = nrg * rows
    m = p['moe'][j]
    x1, h2, idx, gate = _s5_glu_router(xp.reshape(t, d), yf.reshape(t, d), yb.reshape(t, d), gm,
                                       p['s5_d'][j], p['s5_wglu'][j], p['norm_ffn'][i], m['wr'], tm)
    rank, cnt = _moe_plan(idx, tm)
    counts = cnt[0, :N_EXPERTS].astype(jnp.int32)
    ntiles = (counts + tm - 1) // tm
    tile_end = jnp.cumsum(ntiles)
    offs = (tile_end - ntiles) * tm
    e2 = idx[:, :2]
    pos = offs[e2] + rank[:, :2]
    nt_max = (2 * t) // tm + N_EXPERTS
    tile_expert = jnp.minimum(jnp.searchsorted(tile_end, jnp.arange(nt_max, dtype=jnp.int32), side='right'),
                              N_EXPERTS - 1).astype(jnp.int32)
    n_used = tile_end[-1:].astype(jnp.int32)
    xs = _moe_dispatch(h2, pos, nt_max * tm, tm)
    ys = _moe_experts(xs, tile_expert, n_used, m['wg'], m['wu'], m['wd'], tm)
    out = _moe_combine(x1, gate, ys, pos, p['norm_final'], tm, last)
    return out.reshape(nrg, rows, d)


def _to_perm(x, nseg):
    b, s, d = x.shape
    nrg = b * nseg // SUBLANES
    xr = x.reshape(nrg, SUBLANES, s // nseg, d)
    return jnp.swapaxes(xr, 1, 2).reshape(nrg, (s // nseg) * SUBLANES, d)


def _from_perm(xp, b, s, nseg):
    nrg, rows, d = xp.shape
    xr = xp.reshape(nrg, rows // SUBLANES, SUBLANES, d)
    return jnp.swapaxes(xr, 1, 2).reshape(b, s, d)


def _trunk(x, p):
    b, s, d = x.shape
    tm = _tile(s, 512)
    long_seq = s > 4096
    tq = _tile(s, 512 if long_seq else 2048, LANES)
    tk = _tile(s, min(512 if long_seq else 256, s // 4), LANES)
    attn_cfg = (tq, tk, 4 if long_seq else 2, 2, 2 if long_seq else 1)
    nseg = 1 if b % SUBLANES == 0 else SUBLANES // b
    assert (b * nseg) % SUBLANES == 0 and s % nseg == 0
    tt = _tile(s // nseg, 64)
    tabs = _rope_tables(s)
    depth = p['norm_mix'].shape[0]
    for i in range(depth):
        j = i // 2
        if i % 2 == 0:
            x = _mla_layer(x, p, j, tabs, tm, attn_cfg)
        else:
            xp = _s5_moe_layer(_to_perm(x, nseg), p, j, nseg, tt, tm, last=(i == depth - 1))
            x = _from_perm(xp, b, s, nseg)
    if depth % 2 == 1:
        x = _rmsnorm(x.reshape(b * s, d), p['norm_final'], tm).reshape(b, s, d)
    return x


def kernel(x_prompt, x_sample, norm_mix, norm_ffn, norm_final, mla_w_dq, mla_q_norm, mla_w_uq, mla_w_dkv,
           mla_kv_norm, mla_w_ukv, mla_w_o, ssm_a_re, ssm_a_im, ssm_log_step, ssm_b_re, ssm_b_im, ssm_c_re,
           ssm_c_im, ssm_d, ssm_w_glu, ffn_w_gate, ffn_w_up, ffn_w_down, moe_w_router, moe_w_gate, moe_w_up,
           moe_w_down):
    d = x_prompt.shape[-1]
    na, ns = mla_w_dq.shape[0], ssm_a_re.shape[0]
    p = dict(
        norm_mix=norm_mix.reshape(-1, 1, d), norm_ffn=norm_ffn.reshape(-1, 1, d), norm_final=norm_final.reshape(1, d),
        mla=[_prep_mla(mla_w_dq[j], mla_q_norm[j], mla_w_uq[j], mla_w_dkv[j], mla_kv_norm[j], mla_w_ukv[j],
                       mla_w_o[j]) for j in range(na)],
        ffn=[dict(wg=ffn_w_gate[j].astype(BF16), wu=ffn_w_up[j].astype(BF16), wd=ffn_w_down[j].astype(BF16))
             for j in range(na)],
        s5=[_prep_s5(ssm_a_re[j], ssm_a_im[j], ssm_log_step[j], ssm_b_re[j], ssm_b_im[j], ssm_c_re[j], ssm_c_im[j])
            for j in range(ns)],
        s5_d=[ssm_d[j].reshape(1, d) for j in range(ns)],
        s5_wglu=[ssm_w_glu[j].astype(BF16) for j in range(ns)],
        moe=[dict(wr=_split_bf16(jnp.pad(moe_w_router[j], ((0, 0), (0, LANES - N_EXPERTS)))),
                  wg=moe_w_gate[j].astype(BF16), wu=moe_w_up[j].astype(BF16), wd=moe_w_down[j].astype(BF16))
             for j in range(ns)],
    )
    return (_trunk(x_prompt, p), _trunk(x_sample, p))
```
